```python
import math
import jax, jax.numpy as jnp
from jax import lax
import numpy as np

D_MODEL = 1024
BATCH = 8
SEQ = 4096
DEPTH = 2

N_A_LAYERS = DEPTH // 2
N_B_LAYERS = DEPTH - N_A_LAYERS
GLA_HEADS = 4
GLA_DK = D_MODEL // 2
GLA_DV = D_MODEL
GLA_HK = GLA_DK // GLA_HEADS
GLA_HV = GLA_DV // GLA_HEADS
GLA_GATE_RANK = 16
GLA_GATE_NORM = 16.0
GLA_CHUNK = 64
DIFF_HEADS = 8
DIFF_HD = D_MODEL // (2 * DIFF_HEADS)
DIFF_VD = 2 * DIFF_HD
DIFF_QK = DIFF_HEADS * 2 * DIFF_HD
DIFF_V = DIFF_HEADS * DIFF_VD
Q_BLOCK = 128
ROPE_THETA = 10000.0
D_FF = 2816
CONV_WIDTH = 3
EPS = 1e-6

kernel_name = 'yoco_gla_diffattn_convffn'


def rmsnorm(x, w):
    xf = x.astype(jnp.float32)
    y = xf * lax.rsqrt(jnp.mean(xf * xf, axis=-1, keepdims=True) + EPS)
    return (y * w.astype(jnp.float32)).astype(x.dtype)


def rope_tables(seq):
    pos = jnp.arange(seq, dtype=jnp.float32)
    inv = ROPE_THETA ** (-jnp.arange(0, DIFF_HD, 2, dtype=jnp.float32) / DIFF_HD)
    f = pos[:, None] * inv[None, :]
    emb = jnp.concatenate([f, f], axis=-1)
    return jnp.cos(emb), jnp.sin(emb)


def apply_rope(x, cos, sin):
    c = cos[None, :, None, None, :]
    s = sin[None, :, None, None, :]
    x1, x2 = jnp.split(x, 2, axis=-1)
    rot = jnp.concatenate([-x2, x1], axis=-1)
    return (x.astype(jnp.float32) * c + rot.astype(jnp.float32) * s).astype(x.dtype)


def gla_mix(h, w_qkvg, w_gk1, w_gk2, b_gk, onorm_w, w_o):
    B, S, _ = h.shape
    C = GLA_CHUNK
    N = S // C
    proj = h @ w_qkvg
    q, k, v, g = jnp.split(proj, [GLA_DK, 2 * GLA_DK, 2 * GLA_DK + GLA_DV], axis=-1)
    gk = jax.nn.log_sigmoid(((h @ w_gk1) @ w_gk2 + b_gk).astype(jnp.float32)) / GLA_GATE_NORM

    def heads_k(t):
        return t.astype(jnp.float32).reshape(B, N, C, GLA_HEADS, GLA_HK).transpose(0, 3, 1, 2, 4)

    q = heads_k(q) * (GLA_HK ** -0.5)
    k = heads_k(k)
    gk = heads_k(gk)
    v = v.astype(jnp.float32).reshape(B, N, C, GLA_HEADS, GLA_HV).transpose(0, 3, 1, 2, 4)

    b = jnp.cumsum(gk, axis=3)
    b_last = b[:, :, :, -1:, :]
    q_in = q * jnp.exp(b)
    k_in = k * jnp.exp(-b)
    k_end = k * jnp.exp(b_last - b)

    causal = jnp.tril(jnp.ones((C, C), dtype=bool))
    scores = jnp.einsum('bhncd,bhnjd->bhncj', q_in, k_in)
    scores = jnp.where(causal, scores, 0.0)
    o_intra = jnp.einsum('bhncj,bhnje->bhnce', scores, v)

    contrib = jnp.einsum('bhncd,bhnce->bhnde', k_end, v)
    decay = jnp.exp(b_last[:, :, :, 0, :])

    def step(state, inp):
        d, c = inp
        return d[..., None] * state + c, state

    init = jnp.zeros((B, GLA_HEADS, GLA_HK, GLA_HV), jnp.float32)
    _, s_prev = lax.scan(step, init, (jnp.moveaxis(decay, 2, 0), jnp.moveaxis(contrib, 2, 0)))
    s_prev = jnp.moveaxis(s_prev, 0, 2)
    o_inter = jnp.einsum('bhncd,bhnde->bhnce', q_in, s_prev)

    o = (o_intra + o_inter).transpose(0, 2, 3, 1, 4).reshape(B, S, GLA_HEADS, GLA_HV)
    o = rmsnorm(o.astype(h.dtype), onorm_w).reshape(B, S, GLA_DV)
    o = o * jax.nn.silu(g)
    return o @ w_o


def shared_kv(h, kv_norm_w, w_kv, cos, sin):
    B, S, _ = h.shape
    kv = rmsnorm(h, kv_norm_w) @ w_kv
    k, v = jnp.split(kv, [DIFF_QK], axis=-1)
    k = apply_rope(k.reshape(B, S, DIFF_HEADS, 2, DIFF_HD), cos, sin)
    v = v.reshape(B, S, DIFF_HEADS, DIFF_VD)
    return k, v


def diff_attn(h, k_sh, v_sh, w_q, lam_p, subln_w, w_o, lam_init, cos, sin):
    B, S, _ = h.shape
    q = apply_rope((h @ w_q).reshape(B, S, DIFF_HEADS, 2, DIFF_HD), cos, sin)
    lp = lam_p.astype(jnp.float32)
    lam = jnp.exp(jnp.sum(lp[0] * lp[1])) - jnp.exp(jnp.sum(lp[2] * lp[3])) + lam_init
    nqb = S // Q_BLOCK
    qb = jnp.moveaxis(q.reshape(B, nqb, Q_BLOCK, DIFF_HEADS, 2, DIFF_HD), 1, 0)
    kpos = jnp.arange(S)
    scale = DIFF_HD ** -0.5

    def block(args):
        qblk, i = args
        s = jnp.einsum('bqhcd,bkhcd->bhcqk', qblk, k_sh).astype(jnp.float32) * scale
        qpos = i * Q_BLOCK + jnp.arange(Q_BLOCK)
        mask = kpos[None, :] <= qpos[:, None]
        s = jnp.where(mask[None, None, None], s, -jnp.inf)
        p = jax.nn.softmax(s, axis=-1)
        a = p[:, :, 0] - lam * p[:, :, 1]
        return jnp.einsum('bhqk,bkhe->bqhe', a.astype(v_sh.dtype), v_sh)

    o = lax.map(block, (qb, jnp.arange(nqb)))
    o = jnp.moveaxis(o, 0, 1).reshape(B, S, DIFF_HEADS, DIFF_VD)
    o = rmsnorm(o, subln_w) * (1.0 - lam_init)
    return o.reshape(B, S, DIFF_V) @ w_o


def conv_ffn(h, w_in, conv_w, conv_b, w_out):
    S = h.shape[1]
    u = h @ w_in
    up = jnp.pad(u, ((0, 0), (CONV_WIDTH - 1, 0), (0, 0)))
    c = sum(up[:, j:j + S, :] * conv_w[j] for j in range(CONV_WIDTH)) + conv_b
    a, g = jnp.split(c, 2, axis=-1)
    return (jax.nn.silu(g) * a) @ w_out


def setup_inputs(seed: int = 0) -> dict:
    key = jax.random.key(seed)
    ks = jax.random.split(key, 24)

    def nrm(k, shape, scale):
        return jax.random.normal(k, shape, jnp.float32) * scale

    def gain(k, shape):
        return 1.0 + 0.02 * jax.random.normal(k, shape, jnp.float32)

    D = D_MODEL
    return {
        'x': nrm(ks[0], (BATCH, SEQ, D), 1.0),
        'attn_norm_w': gain(ks[1], (DEPTH, D)),
        'ffn_norm_w': gain(ks[2], (DEPTH, D)),
        'gla_w_qkvg': nrm(ks[3], (N_A_LAYERS, D, 2 * GLA_DK + 2 * GLA_DV), D ** -0.5),
        'gla_w_gk1': nrm(ks[4], (N_A_LAYERS, D, GLA_GATE_RANK), D ** -0.5),
        'gla_w_gk2': nrm(ks[5], (N_A_LAYERS, GLA_GATE_RANK, GLA_DK), GLA_GATE_RANK ** -0.5),
        'gla_b_gk': nrm(ks[6], (N_A_LAYERS, GLA_DK), 0.1),
        'gla_onorm_w': gain(ks[7], (N_A_LAYERS, GLA_HV)),
        'gla_w_o': nrm(ks[8], (N_A_LAYERS, GLA_DV, D), GLA_DV ** -0.5),
        'kv_norm_w': gain(ks[9], (D,)),
        'w_kv': nrm(ks[10], (D, DIFF_QK + DIFF_V), D ** -0.5),
        'diff_w_q': nrm(ks[11], (N_B_LAYERS, D, DIFF_QK), D ** -0.5),
        'diff_lambda': nrm(ks[12], (N_B_LAYERS, 4, DIFF_HD), 0.1),
        'diff_subln_w': gain(ks[13], (N_B_LAYERS, DIFF_VD)),
        'diff_w_o': nrm(ks[14], (N_B_LAYERS, DIFF_V, D), DIFF_V ** -0.5),
        'ffn_w_in': nrm(ks[15], (DEPTH, D, 2 * D_FF), D ** -0.5),
        'ffn_conv_w': nrm(ks[16], (DEPTH, CONV_WIDTH, 2 * D_FF), CONV_WIDTH ** -0.5),
        'ffn_conv_b': nrm(ks[17], (DEPTH, 2 * D_FF), 0.02),
        'ffn_w_out': nrm(ks[18], (DEPTH, D_FF, D), D_FF ** -0.5),
        'final_norm_w': gain(ks[19], (D,)),
    }


def reference(x, attn_norm_w, ffn_norm_w, gla_w_qkvg, gla_w_gk1, gla_w_gk2, gla_b_gk, gla_onorm_w, gla_w_o,
              kv_norm_w, w_kv, diff_w_q, diff_lambda, diff_subln_w, diff_w_o,
              ffn_w_in, ffn_conv_w, ffn_conv_b, ffn_w_out, final_norm_w):
    S = x.shape[1]
    cos, sin = rope_tables(S)
    h = x
    k_sh = None
    v_sh = None
    for l in range(DEPTH):
        if l == N_A_LAYERS:
            k_sh, v_sh = shared_kv(h, kv_norm_w, w_kv, cos, sin)
        a_in = rmsnorm(h, attn_norm_w[l])
        if l < N_A_LAYERS:
            h = h + gla_mix(a_in, gla_w_qkvg[l], gla_w_gk1[l], gla_w_gk2[l], gla_b_gk[l], gla_onorm_w[l], gla_w_o[l])
        else:
            j = l - N_A_LAYERS
            lam_init = 0.8 - 0.6 * math.exp(-0.3 * l)
            h = h + diff_attn(a_in, k_sh, v_sh, diff_w_q[j], diff_lambda[j], diff_subln_w[j], diff_w_o[j], lam_init, cos, sin)
        h = h + conv_ffn(rmsnorm(h, ffn_norm_w[l]), ffn_w_in[l], ffn_conv_w[l], ffn_conv_b[l], ffn_w_out[l])
    return rmsnorm(h, final_norm_w)
```

```python
import functools
import math

import jax
import jax.numpy as jnp
from jax import lax
from jax.experimental import pallas as pl
from jax.experimental.pallas import tpu as pltpu

F32 = jnp.float32
BF16 = jnp.bfloat16

EPS = 1e-6
ROPE_THETA = 10000.0
GLA_HEADS = 4
GLA_CHUNK = 64
GLA_GATE_RANK = 16
GLA_GATE_NORM = 16.0
DIFF_HEADS = 8
CONV_WIDTH = 3

LANES = 128
SUBLANES = 8
VMEM_LIMIT = 56 * 1024 * 1024

TOK_BLOCK = 512
FFN_CHUNK = 256
ATTN_BLOCK = 256


def _params(n_axes):
    return pltpu.CompilerParams(
        dimension_semantics=("arbitrary",) * n_axes, vmem_limit_bytes=VMEM_LIMIT)


def _full(shape):
    nd = len(shape)
    return pl.BlockSpec(shape, lambda *_: (0,) * nd)


def _rms(x, w):
    ms = jnp.mean(x * x, axis=-1, keepdims=True)
    return x * lax.rsqrt(ms + EPS) * w


def _sigmoid(x):
    return 1.0 / (1.0 + jnp.exp(-x))


def _dot(a, b):
    return jnp.dot(a, b, preferred_element_type=F32)


def _dot_nt(a, b):
    return lax.dot_general(a, b, (((1,), (1,)), ((), ())), preferred_element_type=F32)


def _dot_tn(a, b):
    return lax.dot_general(a, b, (((0,), (0,)), ((), ())), preferred_element_type=F32)


def _gla_proj_kernel(x_ref, nw_ref, w_ref, w2_ref, b_ref, qkvg_ref, gk_ref, *, n_main):
    hn = _rms(x_ref[...], nw_ref[...]).astype(BF16)
    proj = _dot(hn, w_ref[...])
    qkvg_ref[...] = proj[:, :n_main].astype(BF16)
    low = proj[:, n_main:].astype(BF16)
    z = _dot(low, w2_ref[...]) + b_ref[...]
    gk_ref[...] = (jnp.minimum(z, 0.0) - jnp.log(1.0 + jnp.exp(-jnp.abs(z)))) * (1.0 / GLA_GATE_NORM)


def _gla_proj(x2, nw, w_cat, w2_pad, b_gk):
    T, D = x2.shape
    n_main = w_cat.shape[1] - LANES
    dk = w2_pad.shape[1]
    tm = TOK_BLOCK
    return pl.pallas_call(
        functools.partial(_gla_proj_kernel, n_main=n_main),
        grid=(T // tm,),
        in_specs=[
            pl.BlockSpec((tm, D), lambda i: (i, 0)),
            _full(nw.shape), _full(w_cat.shape), _full(w2_pad.shape), _full(b_gk.shape),
        ],
        out_specs=[
            pl.BlockSpec((tm, n_main), lambda i: (i, 0)),
            pl.BlockSpec((tm, dk), lambda i: (i, 0)),
        ],
        out_shape=[
            jax.ShapeDtypeStruct((T, n_main), BF16),
            jax.ShapeDtypeStruct((T, dk), F32),
        ],
        compiler_params=_params(1),
        name="gla_proj",
    )(x2, nw, w_cat, w2_pad, b_gk)


def _cumsum_rows(g):
    n = g.shape[0]
    ridx = lax.broadcasted_iota(jnp.int32, g.shape, 0)
    b = g
    s = 1
    while s < n:
        b = b + jnp.where(ridx >= s, pltpu.roll(b, s, axis=0), 0.0)
        s *= 2
    return b


def _gla_core_kernel(qkvg_ref, gk_ref, x_ref, onw_ref, wo_ref, out_ref, state_ref, o_scr,
                     *, dk, dv):
    C = GLA_CHUNK
    H = GLA_HEADS
    hk = dk // H
    hv = dv // H
    tb = x_ref.shape[0]

    @pl.when(pl.program_id(1) == 0)
    def _():
        state_ref[...] = jnp.zeros_like(state_ref)

    row = lax.broadcasted_iota(jnp.int32, (C, C), 0)
    col = lax.broadcasted_iota(jnp.int32, (C, C), 1)
    causal = col <= row
    qscale = hk ** -0.5

    for c in range(tb // C):
        r = pl.ds(c * C, C)
        b_all = _cumsum_rows(gk_ref[r, :])
        for h in range(H):
            q = qkvg_ref[r, h * hk:(h + 1) * hk].astype(F32)
            k = qkvg_ref[r, dk + h * hk:dk + (h + 1) * hk].astype(F32)
            v = qkvg_ref[r, 2 * dk + h * hv:2 * dk + (h + 1) * hv]
            b = b_all[:, h * hk:(h + 1) * hk]
            b_last = b[C - 1:C, :]
            q_in = (q * qscale * jnp.exp(b)).astype(BF16)
            k_in = (k * jnp.exp(-b)).astype(BF16)
            k_end = (k * jnp.exp(b_last - b)).astype(BF16)
            scores = jnp.where(causal, _dot_nt(q_in, k_in), 0.0)
            st = state_ref[h]
            o = _dot(scores.astype(BF16), v) + _dot_nt(q_in, st.astype(BF16))
            o_scr[r, h * hv:(h + 1) * hv] = o
            state_ref[h] = st * jnp.exp(b_last) + _dot_tn(v, k_end)

    onw = onw_ref[...]
    parts = []
    for h in range(H):
        parts.append(_rms(o_scr[:, h * hv:(h + 1) * hv], onw))
    on = jnp.concatenate(parts, axis=1)
    g = qkvg_ref[:, 2 * dk + dv:2 * dk + 2 * dv].astype(F32)
    gated = (on * (g * _sigmoid(g))).astype(BF16)
    out_ref[...] = x_ref[...] + _dot(gated, wo_ref[...])


def _gla_core(qkvg3, gk3, x3, onw, wo, dk, dv):
    B, S, D = x3.shape
    tb = TOK_BLOCK
    H = GLA_HEADS
    return pl.pallas_call(
        functools.partial(_gla_core_kernel, dk=dk, dv=dv),
        grid=(B, S // tb),
        in_specs=[
            pl.BlockSpec((None, tb, qkvg3.shape[2]), lambda b, s: (b, s, 0)),
            pl.BlockSpec((None, tb, dk), lambda b, s: (b, s, 0)),
            pl.BlockSpec((None, tb, D), lambda b, s: (b, s, 0)),
            _full(onw.shape), _full(wo.shape),
        ],
        out_specs=pl.BlockSpec((None, tb, D), lambda b, s: (b, s, 0)),
        out_shape=jax.ShapeDtypeStruct((B, S, D), F32),
        scratch_shapes=[
            pltpu.VMEM((H, dv // H, dk // H), F32),
            pltpu.VMEM((tb, dv), F32),
        ],
        compiler_params=_params(2),
        name="gla_core",
    )(qkvg3, gk3, x3, onw, wo)


def _ffn_kernel(*refs, has_mix, final_norm, d_ff):
    refs = list(refs)
    h_ref = refs.pop(0)
    if has_mix:
        mix_ref = refs.pop(0)
        wo_ref = refs.pop(0)
    nw_ref, win_ref, cw_ref, cb_ref, wout_ref = refs[:5]
    refs = refs[5:]
    if final_norm:
        fnw_ref = refs.pop(0)
    out_ref, carry_ref = refs

    tm = h_ref.shape[0]
    fc = FFN_CHUNK

    @pl.when(pl.program_id(1) == 0)
    def _():
        carry_ref[...] = jnp.zeros_like(carry_ref)

    h = h_ref[...]
    if has_mix:
        h = h + _dot(mix_ref[...], wo_ref[...])
    hn = _rms(h, nw_ref[...]).astype(BF16)

    def conv_cols(col0):
        cols = pl.ds(col0, fc)
        u = _dot(hn, win_ref[:, cols])
        ext = jnp.concatenate([carry_ref[:, cols], u], axis=0)
        carry_ref[:, cols] = u[tm - SUBLANES:, :]
        u1 = pltpu.roll(ext, 1, axis=0)[SUBLANES:, :]
        u2 = pltpu.roll(ext, 2, axis=0)[SUBLANES:, :]
        cw = cw_ref[:, cols]
        return u2 * cw[0:1, :] + u1 * cw[1:2, :] + u * cw[2:3, :] + cb_ref[:, cols]

    acc = jnp.zeros((tm, out_ref.shape[1]), F32)
    for j in range(d_ff // fc):
        a = conv_cols(j * fc)
        g = conv_cols(d_ff + j * fc)
        act = (g * _sigmoid(g) * a).astype(BF16)
        acc = acc + _dot(act, wout_ref[pl.ds(j * fc, fc), :])
    out = h + acc
    if final_norm:
        out = _rms(out, fnw_ref[...])
    out_ref[...] = out


def _ffn(h3, nw, w_in, conv_w, conv_b, w_out, mix3=None, wo=None, final_nw=None):
    B, S, D = h3.shape
    tm = TOK_BLOCK
    d_ff = w_out.shape[0]
    has_mix = mix3 is not None
    final_norm = final_nw is not None
    tok = pl.BlockSpec((None, tm, D), lambda b, s: (b, s, 0))
    args, specs = [h3], [tok]
    if has_mix:
        args += [mix3, wo]
        specs += [pl.BlockSpec((None, tm, mix3.shape[2]), lambda b, s: (b, s, 0)), _full(wo.shape)]
    args += [nw, w_in, conv_w, conv_b, w_out]
    specs += [_full(nw.shape), _full(w_in.shape), _full(conv_w.shape), _full(conv_b.shape),
              _full(w_out.shape)]
    if final_norm:
        args.append(final_nw)
        specs.append(_full(final_nw.shape))
    return pl.pallas_call(
        functools.partial(_ffn_kernel, has_mix=has_mix, final_norm=final_norm, d_ff=d_ff),
        grid=(B, S // tm),
        in_specs=specs,
        out_specs=tok,
        out_shape=jax.ShapeDtypeStruct((B, S, D), F32),
        scratch_shapes=[pltpu.VMEM((SUBLANES, 2 * d_ff), F32)],
        compiler_params=_params(2),
        name="ffn_mix" if has_mix else "ffn",
    )(*args)


def _rope(x, cos, sin_lo, sin_hi):
    half = LANES // 4
    return (x * cos + pltpu.roll(x, LANES - half, axis=1) * sin_lo
            + pltpu.roll(x, half, axis=1) * sin_hi)


def _qkv_proj_kernel(h_ref, kvnw_ref, anw_ref, wkv_ref, wq_ref, cos_ref, slo_ref, shi_ref,
                     q_ref, k_ref, v_ref, *, n_qk, q_scale):
    x = h_ref[...]
    ms = jnp.mean(x * x, axis=-1, keepdims=True)
    xn = x * lax.rsqrt(ms + EPS)
    kvn = (xn * kvnw_ref[...]).astype(BF16)
    an = (xn * anw_ref[...]).astype(BF16)
    kv = _dot(kvn, wkv_ref[...])
    q = _dot(an, wq_ref[...])
    cos = cos_ref[...]
    slo = slo_ref[...]
    shi = shi_ref[...]
    for j in range(n_qk // LANES):
        cols = slice(j * LANES, (j + 1) * LANES)
        q_ref[:, cols] = (_rope(q[:, cols], cos, slo, shi) * q_scale).astype(BF16)
        k_ref[:, cols] = _rope(kv[:, cols], cos, slo, shi).astype(BF16)
    v_ref[...] = kv[:, n_qk:].astype(BF16)


def _qkv_proj(h3, kv_nw, a_nw, w_kv, w_q, cos, sin_lo, sin_hi, q_scale):
    B, S, D = h3.shape
    tm = TOK_BLOCK
    n_qk = w_q.shape[1]
    n_v = w_kv.shape[1] - n_qk
    tok = lambda n: pl.BlockSpec((None, tm, n), lambda b, s: (b, s, 0))
    tab = pl.BlockSpec((tm, LANES), lambda b, s: (s, 0))
    return pl.pallas_call(
        functools.partial(_qkv_proj_kernel, n_qk=n_qk, q_scale=q_scale),
        grid=(B, S // tm),
        in_specs=[tok(D), _full(kv_nw.shape), _full(a_nw.shape), _full(w_kv.shape), _full(w_q.shape),
                  tab, tab, tab],
        out_specs=[tok(n_qk), tok(n_qk), tok(n_v)],
        out_shape=[jax.ShapeDtypeStruct((B, S, n_qk), BF16),
                   jax.ShapeDtypeStruct((B, S, n_qk), BF16),
                   jax.ShapeDtypeStruct((B, S, n_v), BF16)],
        compiler_params=_params(2),
        name="qkv_proj",
    )(h3, kv_nw, a_nw, w_kv, w_q, cos, sin_lo, sin_hi)


def _diff_attn_kernel(lam_ref, q_ref, k_ref, v_ref, sw_ref, o_ref, m_scr, l_scr, acc_scr,
                      *, lam_init):
    bq = q_ref.shape[0]
    bk = bq
    hd = LANES // 2
    i = pl.program_id(2)

    q = q_ref[...]
    lane = lax.broadcasted_iota(jnp.int32, q.shape, 1)
    zero = jnp.zeros_like(q)
    qs = jnp.concatenate([jnp.where(lane < hd, q, zero), jnp.where(lane >= hd, q, zero)], axis=0)

    m_scr[...] = jnp.full_like(m_scr, -jnp.inf)
    l_scr[...] = jnp.zeros_like(l_scr)
    acc_scr[...] = jnp.zeros_like(acc_scr)

    def update(j, masked):
        rows = pl.ds(pl.multiple_of(j * bk, bk), bk)
        s = _dot_nt(qs, k_ref[rows, :])
        if masked:
            r = lax.broadcasted_iota(jnp.int32, (bq, bk), 0)
            c = lax.broadcasted_iota(jnp.int32, (bq, bk), 1)
            keep = jnp.concatenate([c <= r, c <= r], axis=0)
            s = jnp.where(keep, s, -jnp.inf)
        m_prev = m_scr[...]
        m_new = jnp.maximum(m_prev, jnp.max(s, axis=-1, keepdims=True))
        alpha = jnp.exp(m_prev - m_new)
        p = jnp.exp(s - m_new)
        l_scr[...] = alpha * l_scr[...] + jnp.sum(p, axis=-1, keepdims=True)
        acc_scr[...] = alpha * acc_scr[...] + _dot(p.astype(BF16), v_ref[rows, :])
        m_scr[...] = m_new

    def body(j, carry):
        update(j, False)
        return carry

    lax.fori_loop(0, i, body, 0)
    update(i, True)

    lp = lam_ref[...]
    lam = (jnp.exp(jnp.sum(lp[0:1, :] * lp[1:2, :], axis=-1, keepdims=True))
           - jnp.exp(jnp.sum(lp[2:3, :] * lp[3:4, :], axis=-1, keepdims=True)) + lam_init)
    o_all = acc_scr[...] / l_scr[...]
    o = o_all[:bq, :] - lam * o_all[bq:, :]
    o_ref[...] = (_rms(o, sw_ref[...]) * (1.0 - lam_init)).astype(BF16)


def _diff_attn(q3, k3, v3, lam_p, subln_w, lam_init):
    B, S, n_qk = q3.shape
    bq = ATTN_BLOCK
    H = DIFF_HEADS
    return pl.pallas_call(
        functools.partial(_diff_attn_kernel, lam_init=lam_init),
        grid=(B, H, S // bq),
        in_specs=[
            _full(lam_p.shape),
            pl.BlockSpec((None, bq, LANES), lambda b, h, i: (b, i, h)),
            pl.BlockSpec((None, S, LANES), lambda b, h, i: (b, 0, h)),
            pl.BlockSpec((None, S, LANES), lambda b, h, i: (b, 0, h)),
            _full(subln_w.shape),
        ],
        out_specs=pl.BlockSpec((None, bq, LANES), lambda b, h, i: (b, i, h)),
        out_shape=jax.ShapeDtypeStruct((B, S, H * LANES), BF16),
        scratch_shapes=[
            pltpu.VMEM((2 * bq, 1), F32),
            pltpu.VMEM((2 * bq, 1), F32),
            pltpu.VMEM((2 * bq, LANES), F32),
        ],
        compiler_params=_params(3),
        name="diff_attn",
    )(lam_p, q3, k3, v3, subln_w)


def _rope_tables(seq, hd):
    pos = jnp.arange(seq, dtype=F32)
    inv = ROPE_THETA ** (-jnp.arange(0, hd, 2, dtype=F32) / hd)
    f = pos[:, None] * inv[None, :]
    cos_h, sin_h = jnp.cos(f), jnp.sin(f)
    zeros = jnp.zeros_like(sin_h)
    cos = jnp.concatenate([cos_h, cos_h] * 2, axis=-1)
    sin_lo = jnp.concatenate([-sin_h, zeros] * 2, axis=-1)
    sin_hi = jnp.concatenate([zeros, sin_h] * 2, axis=-1)
    return cos, sin_lo, sin_hi


def kernel(x, attn_norm_w, ffn_norm_w, gla_w_qkvg, gla_w_gk1, gla_w_gk2, gla_b_gk, gla_onorm_w, gla_w_o, kv_norm_w, w_kv, diff_w_q, diff_lambda, diff_subln_w, diff_w_o, ffn_w_in, ffn_conv_w, ffn_conv_b, ffn_w_out, final_norm_w):
    B, S, D = x.shape
    depth = attn_norm_w.shape[0]
    n_a = gla_w_qkvg.shape[0]
    dk = gla_w_gk2.shape[2]
    dv = gla_w_o.shape[1]
    n_qk = diff_w_q.shape[2]
    hd = n_qk // (2 * DIFF_HEADS)
    assert hd * 2 == LANES and (w_kv.shape[1] - n_qk) == DIFF_HEADS * LANES
    assert S % TOK_BLOCK == 0 and S % ATTN_BLOCK == 0 and TOK_BLOCK % GLA_CHUNK == 0
    assert ffn_w_out.shape[1] % FFN_CHUNK == 0 and GLA_GATE_RANK <= LANES

    row = lambda v: v.reshape(1, -1)
    cos, sin_lo, sin_hi = _rope_tables(S, hd)

    h = x
    q3 = k3 = v3 = None
    for l in range(depth):
        final_nw = row(final_norm_w) if l == depth - 1 else None
        ffn_args = (row(ffn_norm_w[l]), ffn_w_in[l].astype(BF16), ffn_conv_w[l], row(ffn_conv_b[l]),
                    ffn_w_out[l].astype(BF16))
        if l < n_a:
            w1_pad = jnp.pad(gla_w_gk1[l], ((0, 0), (0, LANES - GLA_GATE_RANK)))
            w_cat = jnp.concatenate([gla_w_qkvg[l], w1_pad], axis=1).astype(BF16)
            w2_pad = jnp.pad(gla_w_gk2[l], ((0, LANES - GLA_GATE_RANK), (0, 0))).astype(BF16)
            qkvg, gk = _gla_proj(h.reshape(B * S, D), row(attn_norm_w[l]), w_cat, w2_pad,
                                 row(gla_b_gk[l]))
            h = _gla_core(qkvg.reshape(B, S, -1), gk.reshape(B, S, dk), h, row(gla_onorm_w[l]),
                          gla_w_o[l].astype(BF16), dk, dv)
            h = _ffn(h, *ffn_args, final_nw=final_nw)
        else:
            j = l - n_a
            if l == n_a:
                pass
            lam_init = 0.8 - 0.6 * math.exp(-0.3 * l)
            q3, k_new, v_new = _qkv_proj(h, row(kv_norm_w), row(attn_norm_w[l]), w_kv.astype(BF16),
                                         diff_w_q[j].astype(BF16), cos, sin_lo, sin_hi,
                                         float(hd) ** -0.5)
            if l == n_a:
                k3, v3 = k_new, v_new
            o3 = _diff_attn(q3, k3, v3, diff_lambda[j], row(diff_subln_w[j]), lam_init)
            h = _ffn(h, *ffn_args, mix3=o3, wo=diff_w_o[j].astype(BF16), final_nw=final_nw)
    return h
```

```python
import functools
import math

import jax
import jax.numpy as jnp
from jax import lax
from jax.experimental import pallas as pl
from jax.experimental.pallas import tpu as pltpu

F32 = jnp.float32
BF16 = jnp.bfloat16

EPS = 1e-6
ROPE_THETA = 10000.0
GLA_HEADS = 4
GLA_CHUNK = 64
GLA_GATE_RANK = 16
GLA_GATE_NORM = 16.0
DIFF_HEADS = 8
CONV_WIDTH = 3

LANES = 128
SUBLANES = 8
VMEM_LIMIT = 56 * 1024 * 1024

TOK_BLOCK = 512
FFN_CHUNK = 256
ATTN_BLOCK = 512
LOG2E = math.log2(math.e)


def _params(n_axes):
    return pltpu.CompilerParams(
        dimension_semantics=("arbitrary",) * n_axes, vmem_limit_bytes=VMEM_LIMIT)


def _full(shape):
    nd = len(shape)
    return pl.BlockSpec(shape, lambda *_: (0,) * nd)


def _rms(x, w):
    ms = jnp.mean(x * x, axis=-1, keepdims=True)
    return x * lax.rsqrt(ms + EPS) * w


def _sigmoid(x):
    return 1.0 / (1.0 + jnp.exp(-x))


def _dot(a, b):
    return jnp.dot(a, b, preferred_element_type=F32)


def _dot_nt(a, b):
    return lax.dot_general(a, b, (((1,), (1,)), ((), ())), preferred_element_type=F32)


def _dot_tn(a, b):
    return lax.dot_general(a, b, (((0,), (0,)), ((), ())), preferred_element_type=F32)


def _gla_proj_kernel(x_ref, nw_ref, w_ref, w2_ref, b_ref, qkvg_ref, gk_ref, *, n_main):
    hn = _rms(x_ref[...], nw_ref[...]).astype(BF16)
    proj = _dot(hn, w_ref[...])
    qkvg_ref[...] = proj[:, :n_main].astype(BF16)
    low = proj[:, n_main:].astype(BF16)
    z = _dot(low, w2_ref[...]) + b_ref[...]
    gk_ref[...] = (jnp.minimum(z, 0.0) - jnp.log(1.0 + jnp.exp(-jnp.abs(z)))) * (1.0 / GLA_GATE_NORM)


def _gla_proj(x2, nw, w_cat, w2_pad, b_gk):
    T, D = x2.shape
    n_main = w_cat.shape[1] - LANES
    dk = w2_pad.shape[1]
    tm = TOK_BLOCK
    return pl.pallas_call(
        functools.partial(_gla_proj_kernel, n_main=n_main),
        grid=(T // tm,),
        in_specs=[
            pl.BlockSpec((tm, D), lambda i: (i, 0)),
            _full(nw.shape), _full(w_cat.shape), _full(w2_pad.shape), _full(b_gk.shape),
        ],
        out_specs=[
            pl.BlockSpec((tm, n_main), lambda i: (i, 0)),
            pl.BlockSpec((tm, dk), lambda i: (i, 0)),
        ],
        out_shape=[
            jax.ShapeDtypeStruct((T, n_main), BF16),
            jax.ShapeDtypeStruct((T, dk), F32),
        ],
        compiler_params=_params(1),
        name="gla_proj",
    )(x2, nw, w_cat, w2_pad, b_gk)


def _cumsum_rows(g):
    n = g.shape[0]
    ridx = lax.broadcasted_iota(jnp.int32, g.shape, 0)
    b = g
    s = 1
    while s < n:
        b = b + jnp.where(ridx >= s, pltpu.roll(b, s, axis=0), 0.0)
        s *= 2
    return b


def _gla_core_kernel(qkvg_ref, gk_ref, x_ref, onw_ref, wo_ref, out_ref, state_ref, o_scr,
                     *, dk, dv):
    C = GLA_CHUNK
    H = GLA_HEADS
    hk = dk // H
    hv = dv // H
    tb = x_ref.shape[0]

    @pl.when(pl.program_id(1) == 0)
    def _():
        state_ref[...] = jnp.zeros_like(state_ref)

    row = lax.broadcasted_iota(jnp.int32, (C, C), 0)
    col = lax.broadcasted_iota(jnp.int32, (C, C), 1)
    causal = col <= row
    qscale = hk ** -0.5

    for c in range(tb // C):
        r = pl.ds(c * C, C)
        b_all = _cumsum_rows(gk_ref[r, :])
        for h in range(H):
            q = qkvg_ref[r, h * hk:(h + 1) * hk].astype(F32)
            k = qkvg_ref[r, dk + h * hk:dk + (h + 1) * hk].astype(F32)
            v = qkvg_ref[r, 2 * dk + h * hv:2 * dk + (h + 1) * hv]
            b = b_all[:, h * hk:(h + 1) * hk]
            b_last = b[C - 1:C, :]
            q_in = (q * qscale * jnp.exp(b)).astype(BF16)
            k_in = (k * jnp.exp(-b)).astype(BF16)
            k_end = (k * jnp.exp(b_last - b)).astype(BF16)
            scores = jnp.where(causal, _dot_nt(q_in, k_in), 0.0)
            st = state_ref[h]
            o = _dot(scores.astype(BF16), v) + _dot_nt(q_in, st.astype(BF16))
            o_scr[r, h * hv:(h + 1) * hv] = o
            state_ref[h] = st * jnp.exp(b_last) + _dot_tn(v, k_end)

    onw = onw_ref[...]
    parts = []
    for h in range(H):
        parts.append(_rms(o_scr[:, h * hv:(h + 1) * hv], onw))
    on = jnp.concatenate(parts, axis=1)
    g = qkvg_ref[:, 2 * dk + dv:2 * dk + 2 * dv].astype(F32)
    gated = (on * (g * _sigmoid(g))).astype(BF16)
    out_ref[...] = x_ref[...] + _dot(gated, wo_ref[...])


def _gla_core(qkvg3, gk3, x3, onw, wo, dk, dv):
    B, S, D = x3.shape
    tb = TOK_BLOCK
    H = GLA_HEADS
    return pl.pallas_call(
        functools.partial(_gla_core_kernel, dk=dk, dv=dv),
        grid=(B, S // tb),
        in_specs=[
            pl.BlockSpec((None, tb, qkvg3.shape[2]), lambda b, s: (b, s, 0)),
            pl.BlockSpec((None, tb, dk), lambda b, s: (b, s, 0)),
            pl.BlockSpec((None, tb, D), lambda b, s: (b, s, 0)),
            _full(onw.shape), _full(wo.shape),
        ],
        out_specs=pl.BlockSpec((None, tb, D), lambda b, s: (b, s, 0)),
        out_shape=jax.ShapeDtypeStruct((B, S, D), F32),
        scratch_shapes=[
            pltpu.VMEM((H, dv // H, dk // H), F32),
            pltpu.VMEM((tb, dv), F32),
        ],
        compiler_params=_params(2),
        name="gla_core",
    )(qkvg3, gk3, x3, onw, wo)


def _ffn_kernel(*refs, has_mix, final_norm, d_ff):
    refs = list(refs)
    h_ref = refs.pop(0)
    if has_mix:
        mix_ref = refs.pop(0)
        wo_ref = refs.pop(0)
    nw_ref, win_ref, cw_ref, cb_ref, wout_ref = refs[:5]
    refs = refs[5:]
    if final_norm:
        fnw_ref = refs.pop(0)
    out_ref, carry_ref = refs

    tm = h_ref.shape[0]
    fc = FFN_CHUNK

    @pl.when(pl.program_id(1) == 0)
    def _():
        carry_ref[...] = jnp.zeros_like(carry_ref)

    h = h_ref[...]
    if has_mix:
        h = h + _dot(mix_ref[...], wo_ref[...])
    hn = _rms(h, nw_ref[...]).astype(BF16)

    def conv_cols(col0):
        cols = pl.ds(col0, fc)
        u = _dot(hn, win_ref[:, cols])
        ext = jnp.concatenate([carry_ref[:, cols], u], axis=0)
        carry_ref[:, cols] = u[tm - SUBLANES:, :]
        u1 = pltpu.roll(ext, 1, axis=0)[SUBLANES:, :]
        u2 = pltpu.roll(ext, 2, axis=0)[SUBLANES:, :]
        cw = cw_ref[:, cols]
        return u2 * cw[0:1, :] + u1 * cw[1:2, :] + u * cw[2:3, :] + cb_ref[:, cols]

    acc = jnp.zeros((tm, out_ref.shape[1]), F32)
    for j in range(d_ff // fc):
        a = conv_cols(j * fc)
        g = conv_cols(d_ff + j * fc)
        act = (g * _sigmoid(g) * a).astype(BF16)
        acc = acc + _dot(act, wout_ref[pl.ds(j * fc, fc), :])
    out = h + acc
    if final_norm:
        out = _rms(out, fnw_ref[...])
    out_ref[...] = out


def _ffn(h3, nw, w_in, conv_w, conv_b, w_out, mix3=None, wo=None, final_nw=None):
    B, S, D = h3.shape
    tm = TOK_BLOCK
    d_ff = w_out.shape[0]
    has_mix = mix3 is not None
    final_norm = final_nw is not None
    tok = pl.BlockSpec((None, tm, D), lambda b, s: (b, s, 0))
    args, specs = [h3], [tok]
    if has_mix:
        args += [mix3, wo]
        specs += [pl.BlockSpec((None, tm, mix3.shape[2]), lambda b, s: (b, s, 0)), _full(wo.shape)]
    args += [nw, w_in, conv_w, conv_b, w_out]
    specs += [_full(nw.shape), _full(w_in.shape), _full(conv_w.shape), _full(conv_b.shape),
              _full(w_out.shape)]
    if final_norm:
        args.append(final_nw)
        specs.append(_full(final_nw.shape))
    return pl.pallas_call(
        functools.partial(_ffn_kernel, has_mix=has_mix, final_norm=final_norm, d_ff=d_ff),
        grid=(B, S // tm),
        in_specs=specs,
        out_specs=tok,
        out_shape=jax.ShapeDtypeStruct((B, S, D), F32),
        scratch_shapes=[pltpu.VMEM((SUBLANES, 2 * d_ff), F32)],
        compiler_params=_params(2),
        name="ffn_mix" if has_mix else "ffn",
    )(*args)


def _rope(x, cos, sin_lo, sin_hi):
    half = LANES // 4
    return (x * cos + pltpu.roll(x, LANES - half, axis=1) * sin_lo
            + pltpu.roll(x, half, axis=1) * sin_hi)


def _qkv_proj_kernel(h_ref, kvnw_ref, anw_ref, wk_ref, wvt_ref, wq_ref, cos_ref, slo_ref, shi_ref,
                     q_ref, k_ref, vt_ref, *, q_scale):
    x = h_ref[...]
    ms = jnp.mean(x * x, axis=-1, keepdims=True)
    xn = x * lax.rsqrt(ms + EPS)
    kvn = (xn * kvnw_ref[...]).astype(BF16)
    an = (xn * anw_ref[...]).astype(BF16)
    k = _dot(kvn, wk_ref[...])
    q = _dot(an, wq_ref[...])
    vt = _dot_nt(wvt_ref[...], kvn)
    cos = cos_ref[...]
    slo = slo_ref[...]
    shi = shi_ref[...]
    for j in range(q.shape[1] // LANES):
        cols = slice(j * LANES, (j + 1) * LANES)
        q_ref[:, cols] = (_rope(q[:, cols], cos, slo, shi) * q_scale).astype(BF16)
        k_ref[:, cols] = _rope(k[:, cols], cos, slo, shi).astype(BF16)
    ones = jnp.ones((SUBLANES, vt.shape[1]), BF16)
    for h in range(vt_ref.shape[0]):
        vt_ref[h, :LANES, :] = vt[h * LANES:(h + 1) * LANES, :].astype(BF16)
        vt_ref[h, LANES:, :] = ones


def _qkv_proj(h3, kv_nw, a_nw, w_k, w_vt, w_q, cos, sin_lo, sin_hi, q_scale):
    B, S, D = h3.shape
    tm = TOK_BLOCK
    n_qk = w_q.shape[1]
    H = w_vt.shape[0] // LANES
    tok = lambda n: pl.BlockSpec((None, tm, n), lambda b, s: (b, s, 0))
    tab = pl.BlockSpec((tm, LANES), lambda b, s: (s, 0))
    return pl.pallas_call(
        functools.partial(_qkv_proj_kernel, q_scale=q_scale),
        grid=(B, S // tm),
        in_specs=[tok(D), _full(kv_nw.shape), _full(a_nw.shape), _full(w_k.shape), _full(w_vt.shape),
                  _full(w_q.shape), tab, tab, tab],
        out_specs=[tok(n_qk), tok(n_qk),
                   pl.BlockSpec((None, H, LANES + SUBLANES, tm), lambda b, s: (b, 0, 0, s))],
        out_shape=[jax.ShapeDtypeStruct((B, S, n_qk), BF16),
                   jax.ShapeDtypeStruct((B, S, n_qk), BF16),
                   jax.ShapeDtypeStruct((B, H, LANES + SUBLANES, S), BF16)],
        compiler_params=_params(2),
        name="qkv_proj",
    )(h3, kv_nw, a_nw, w_k, w_vt, w_q, cos, sin_lo, sin_hi)


def _diff_attn_kernel(lam_ref, q_ref, k_ref, vt_ref, swt_ref, o_ref,
                      s0_scr, s1_scr, p0_scr, p1_scr, a0_scr, a1_scr, m_scr, acc_scr, *, lam_init):
    bq = q_ref.shape[0]
    bk = bq
    hd = LANES // 2
    i = pl.program_id(2)
    s_bufs = (s0_scr, s1_scr)
    p_bufs = (p0_scr, p1_scr)
    a_bufs = (a0_scr, a1_scr)

    q = q_ref[...]
    lane = lax.broadcasted_iota(jnp.int32, q.shape, 1)
    zero = jnp.zeros_like(q)
    qs = jnp.concatenate([jnp.where(lane < hd, q, zero), jnp.where(lane >= hd, q, zero)], axis=0)

    m_scr[...] = jnp.full_like(m_scr, -jnp.inf)
    acc_scr[...] = jnp.zeros_like(acc_scr)

    def qk(n, slot):
        rows = pl.ds(pl.multiple_of(n * bk, bk), bk)
        s_bufs[slot][...] = _dot_nt(k_ref[rows, :], qs)

    def softmax(slot, masked):
        s = s_bufs[slot][...]
        if masked:
            kk = lax.broadcasted_iota(jnp.int32, (bk, bq), 0)
            qq = lax.broadcasted_iota(jnp.int32, (bk, bq), 1)
            keep = kk <= qq
            s = jnp.where(jnp.concatenate([keep, keep], axis=1), s, -jnp.inf)
        m_prev = m_scr[...]
        m_new = jnp.maximum(m_prev, jnp.max(s, axis=0, keepdims=True))
        a_bufs[slot][...] = jnp.exp2(m_prev - m_new)
        p_bufs[slot][...] = jnp.exp2(s - m_new).astype(BF16)
        m_scr[...] = m_new

    def pv(n, slot):
        cols = pl.ds(pl.multiple_of(n * bk, bk), bk)
        acc_scr[...] = a_bufs[slot][...] * acc_scr[...] + _dot(vt_ref[:, cols], p_bufs[slot][...])

    def stage(n, slot):
        pv(n - 1, 1 - slot)
        qk(n + 1, 1 - slot)
        softmax(slot, False)

    qk(0, 0)

    @pl.when(i > 0)
    def _():
        qk(1, 1)
        softmax(0, False)

    def body(n, carry):
        lax.cond(n % 2 == 1, lambda: stage(n, 1), lambda: stage(n, 0))
        return carry

    lax.fori_loop(1, i, body, 0)

    @pl.when(i % 2 == 1)
    def _():
        pv(i - 1, 0)
        softmax(1, True)
        pv(i, 1)

    @pl.when(i % 2 == 0)
    def _():
        @pl.when(i > 0)
        def _():
            pv(i - 1, 1)

        softmax(0, True)
        pv(i, 0)

    lp = lam_ref[...]
    lam = (jnp.exp(jnp.sum(lp[0:1, :] * lp[1:2, :], axis=-1, keepdims=True))
           - jnp.exp(jnp.sum(lp[2:3, :] * lp[3:4, :], axis=-1, keepdims=True)) + lam_init)
    acc = acc_scr[...]
    o_all = acc[:LANES, :] / acc[LANES:LANES + 1, :]
    ot = o_all[:, :bq] - lam * o_all[:, bq:]
    ms = jnp.mean(ot * ot, axis=0, keepdims=True)
    ot = ot * lax.rsqrt(ms + EPS) * swt_ref[...] * (1.0 - lam_init)
    o_ref[...] = ot.T.astype(BF16)


def _diff_attn(q3, k3, vt4, lam_p, subln_col, lam_init):
    B, S, n_qk = q3.shape
    bq = ATTN_BLOCK
    H = DIFF_HEADS
    return pl.pallas_call(
        functools.partial(_diff_attn_kernel, lam_init=lam_init),
        grid=(B, H, S // bq),
        in_specs=[
            _full(lam_p.shape),
            pl.BlockSpec((None, bq, LANES), lambda b, h, i: (b, i, h)),
            pl.BlockSpec((None, S, LANES), lambda b, h, i: (b, 0, h)),
            pl.BlockSpec((None, None, LANES + SUBLANES, S), lambda b, h, i: (b, h, 0, 0)),
            _full(subln_col.shape),
        ],
        out_specs=pl.BlockSpec((None, bq, LANES), lambda b, h, i: (b, i, h)),
        out_shape=jax.ShapeDtypeStruct((B, S, H * LANES), BF16),
        scratch_shapes=[
            pltpu.VMEM((bq, 2 * bq), F32),
            pltpu.VMEM((bq, 2 * bq), F32),
            pltpu.VMEM((bq, 2 * bq), BF16),
            pltpu.VMEM((bq, 2 * bq), BF16),
            pltpu.VMEM((1, 2 * bq), F32),
            pltpu.VMEM((1, 2 * bq), F32),
            pltpu.VMEM((1, 2 * bq), F32),
            pltpu.VMEM((LANES + SUBLANES, 2 * bq), F32),
        ],
        compiler_params=_params(3),
        name="diff_attn",
    )(lam_p, q3, k3, vt4, subln_col)


def _rope_tables(seq, hd):
    pos = jnp.arange(seq, dtype=F32)
    inv = ROPE_THETA ** (-jnp.arange(0, hd, 2, dtype=F32) / hd)
    f = pos[:, None] * inv[None, :]
    cos_h, sin_h = jnp.cos(f), jnp.sin(f)
    zeros = jnp.zeros_like(sin_h)
    cos = jnp.concatenate([cos_h, cos_h] * 2, axis=-1)
    sin_lo = jnp.concatenate([-sin_h, zeros] * 2, axis=-1)
    sin_hi = jnp.concatenate([zeros, sin_h] * 2, axis=-1)
    return cos, sin_lo, sin_hi


def kernel(x, attn_norm_w, ffn_norm_w, gla_w_qkvg, gla_w_gk1, gla_w_gk2, gla_b_gk, gla_onorm_w, gla_w_o, kv_norm_w, w_kv, diff_w_q, diff_lambda, diff_subln_w, diff_w_o, ffn_w_in, ffn_conv_w, ffn_conv_b, ffn_w_out, final_norm_w):
    B, S, D = x.shape
    depth = attn_norm_w.shape[0]
    n_a = gla_w_qkvg.shape[0]
    dk = gla_w_gk2.shape[2]
    dv = gla_w_o.shape[1]
    n_qk = diff_w_q.shape[2]
    hd = n_qk // (2 * DIFF_HEADS)
    assert hd * 2 == LANES and (w_kv.shape[1] - n_qk) == DIFF_HEADS * LANES
    assert S % TOK_BLOCK == 0 and S % ATTN_BLOCK == 0 and TOK_BLOCK % GLA_CHUNK == 0
    assert ffn_w_out.shape[1] % FFN_CHUNK == 0 and GLA_GATE_RANK <= LANES

    row = lambda v: v.reshape(1, -1)
    cos, sin_lo, sin_hi = _rope_tables(S, hd)

    h = x
    for l in range(depth):
        final_nw = row(final_norm_w) if l == depth - 1 else None
        ffn_args = (row(ffn_norm_w[l]), ffn_w_in[l].astype(BF16), ffn_conv_w[l], row(ffn_conv_b[l]),
                    ffn_w_out[l].astype(BF16))
        if l < n_a:
            w1_pad = jnp.pad(gla_w_gk1[l], ((0, 0), (0, LANES - GLA_GATE_RANK)))
            w_cat = jnp.concatenate([gla_w_qkvg[l], w1_pad], axis=1).astype(BF16)
            w2_pad = jnp.pad(gla_w_gk2[l], ((0, LANES - GLA_GATE_RANK), (0, 0))).astype(BF16)
            qkvg, gk = _gla_proj(h.reshape(B * S, D), row(attn_norm_w[l]), w_cat, w2_pad,
                                 row(gla_b_gk[l]))
            h = _gla_core(qkvg.reshape(B, S, -1), gk.reshape(B, S, dk), h, row(gla_onorm_w[l]),
                          gla_w_o[l].astype(BF16), dk, dv)
            h = _ffn(h, *ffn_args, final_nw=final_nw)
        else:
            j = l - n_a
            lam_init = 0.8 - 0.6 * math.exp(-0.3 * l)
            q3, k_new, vt_new = _qkv_proj(h, row(kv_norm_w), row(attn_norm_w[l]),
                                          w_kv[:, :n_qk].astype(BF16), w_kv[:, n_qk:].T.astype(BF16),
                                          diff_w_q[j].astype(BF16), cos, sin_lo, sin_hi,
                                          float(hd) ** -0.5 * LOG2E)
            if l == n_a:
                k3, vt4 = k_new, vt_new
            o3 = _diff_attn(q3, k3, vt4, diff_lambda[j], diff_subln_w[j].reshape(-1, 1), lam_init)
            h = _ffn(h, *ffn_args, mix3=o3, wo=diff_w_o[j].astype(BF16), final_nw=final_nw)
    return h
```

```python
import functools
import math

import jax
import jax.numpy as jnp
from jax import lax
from jax.experimental import pallas as pl
from jax.experimental.pallas import tpu as pltpu

F32 = jnp.float32
BF16 = jnp.bfloat16

EPS = 1e-6
ROPE_THETA = 10000.0
GLA_HEADS = 4
GLA_CHUNK = 64
GLA_GATE_RANK = 16
GLA_GATE_NORM = 16.0
DIFF_HEADS = 8
CONV_WIDTH = 3

LANES = 128
SUBLANES = 8
VMEM_LIMIT = 56 * 1024 * 1024

TOK_BLOCK = 512
FFN_CHUNK = 256
FFN_DOWN_GROUP = 4
ATTN_BLOCK = 512
LOG2E = math.log2(math.e)


def _params(n_axes):
    return pltpu.CompilerParams(
        dimension_semantics=("arbitrary",) * n_axes, vmem_limit_bytes=VMEM_LIMIT)


def _full(shape):
    nd = len(shape)
    return pl.BlockSpec(shape, lambda *_: (0,) * nd)


def _rms(x, w):
    ms = jnp.mean(x * x, axis=-1, keepdims=True)
    return x * lax.rsqrt(ms + EPS) * w


def _sigmoid(x):
    return 1.0 / (1.0 + jnp.exp2(x * (-LOG2E)))


def _dot(a, b):
    return jnp.dot(a, b, preferred_element_type=F32)


def _dot_nt(a, b):
    return lax.dot_general(a, b, (((1,), (1,)), ((), ())), preferred_element_type=F32)


def _dot_tn(a, b):
    return lax.dot_general(a, b, (((0,), (0,)), ((), ())), preferred_element_type=F32)


def _gla_proj_kernel(x_ref, nw_ref, w_ref, w2_ref, b_ref, qkvg_ref, gk_ref, *, n_main):
    hn = _rms(x_ref[...], nw_ref[...]).astype(BF16)
    proj = _dot(hn, w_ref[...])
    qkvg_ref[...] = proj[:, :n_main].astype(BF16)
    low = proj[:, n_main:].astype(BF16)
    z = _dot(low, w2_ref[...]) + b_ref[...]
    gk_ref[...] = (jnp.minimum(z, 0.0) - jnp.log(1.0 + jnp.exp(-jnp.abs(z)))) * (1.0 / GLA_GATE_NORM)


def _gla_proj(x2, nw, w_cat, w2_pad, b_gk):
    T, D = x2.shape
    n_main = w_cat.shape[1] - LANES
    dk = w2_pad.shape[1]
    tm = TOK_BLOCK
    return pl.pallas_call(
        functools.partial(_gla_proj_kernel, n_main=n_main),
        grid=(T // tm,),
        in_specs=[
            pl.BlockSpec((tm, D), lambda i: (i, 0)),
            _full(nw.shape), _full(w_cat.shape), _full(w2_pad.shape), _full(b_gk.shape),
        ],
        out_specs=[
            pl.BlockSpec((tm, n_main), lambda i: (i, 0)),
            pl.BlockSpec((tm, dk), lambda i: (i, 0)),
        ],
        out_shape=[
            jax.ShapeDtypeStruct((T, n_main), BF16),
            jax.ShapeDtypeStruct((T, dk), F32),
        ],
        compiler_params=_params(1),
        name="gla_proj",
    )(x2, nw, w_cat, w2_pad, b_gk)


def _cumsum_rows(g):
    n = g.shape[0]
    ridx = lax.broadcasted_iota(jnp.int32, g.shape, 0)
    b = g
    s = 1
    while s < n:
        b = b + jnp.where(ridx >= s, pltpu.roll(b, s, axis=0), 0.0)
        s *= 2
    return b


def _gla_core_kernel(qkvg_ref, gk_ref, x_ref, onw_ref, wo_ref, out_ref, state_ref, o_scr,
                     *, dk, dv):
    C = GLA_CHUNK
    H = GLA_HEADS
    hk = dk // H
    hv = dv // H
    tb = x_ref.shape[0]

    @pl.when(pl.program_id(1) == 0)
    def _():
        state_ref[...] = jnp.zeros_like(state_ref)

    row = lax.broadcasted_iota(jnp.int32, (C, C), 0)
    col = lax.broadcasted_iota(jnp.int32, (C, C), 1)
    causal = col <= row
    qscale = hk ** -0.5

    for c in range(tb // C):
        r = pl.ds(c * C, C)
        b_all = _cumsum_rows(gk_ref[r, :])
        for h in range(H):
            q = qkvg_ref[r, h * hk:(h + 1) * hk].astype(F32)
            k = qkvg_ref[r, dk + h * hk:dk + (h + 1) * hk].astype(F32)
            v = qkvg_ref[r, 2 * dk + h * hv:2 * dk + (h + 1) * hv]
            b = b_all[:, h * hk:(h + 1) * hk]
            b_last = b[C - 1:C, :]
            q_in = (q * qscale * jnp.exp(b)).astype(BF16)
            k_in = (k * jnp.exp(-b)).astype(BF16)
            k_end = (k * jnp.exp(b_last - b)).astype(BF16)
            scores = jnp.where(causal, _dot_nt(q_in, k_in), 0.0)
            st = state_ref[h]
            o = _dot(scores.astype(BF16), v) + _dot_nt(q_in, st.astype(BF16))
            o_scr[r, h * hv:(h + 1) * hv] = o
            state_ref[h] = st * jnp.exp(b_last) + _dot_tn(v, k_end)

    onw = onw_ref[...]
    parts = []
    for h in range(H):
        parts.append(_rms(o_scr[:, h * hv:(h + 1) * hv], onw))
    on = jnp.concatenate(parts, axis=1)
    g = qkvg_ref[:, 2 * dk + dv:2 * dk + 2 * dv].astype(F32)
    gated = (on * (g * _sigmoid(g))).astype(BF16)
    out_ref[...] = x_ref[...] + _dot(gated, wo_ref[...])


def _gla_core(qkvg3, gk3, x3, onw, wo, dk, dv):
    B, S, D = x3.shape
    tb = TOK_BLOCK
    H = GLA_HEADS
    return pl.pallas_call(
        functools.partial(_gla_core_kernel, dk=dk, dv=dv),
        grid=(B, S // tb),
        in_specs=[
            pl.BlockSpec((None, tb, qkvg3.shape[2]), lambda b, s: (b, s, 0)),
            pl.BlockSpec((None, tb, dk), lambda b, s: (b, s, 0)),
            pl.BlockSpec((None, tb, D), lambda b, s: (b, s, 0)),
            _full(onw.shape), _full(wo.shape),
        ],
        out_specs=pl.BlockSpec((None, tb, D), lambda b, s: (b, s, 0)),
        out_shape=jax.ShapeDtypeStruct((B, S, D), F32),
        scratch_shapes=[
            pltpu.VMEM((H, dv // H, dk // H), F32),
            pltpu.VMEM((tb, dv), F32),
        ],
        compiler_params=_params(2),
        name="gla_core",
    )(qkvg3, gk3, x3, onw, wo)


def _ffn_kernel(*refs, has_mix, final_norm, d_ff):
    refs = list(refs)
    h_ref = refs.pop(0)
    if has_mix:
        mix_ref = refs.pop(0)
        wo_ref = refs.pop(0)
    nw_ref, win_ref, cw_ref, cb_ref, wout_ref = refs[:5]
    refs = refs[5:]
    if final_norm:
        fnw_ref = refs.pop(0)
    out_ref, u_ref, carry_ref, act_ref = refs

    tm = h_ref.shape[0]
    fc = FFN_CHUNK
    pad = SUBLANES

    @pl.when(pl.program_id(1) == 0)
    def _():
        carry_ref[...] = jnp.zeros_like(carry_ref)

    h = h_ref[...]
    if has_mix:
        h = h + _dot(mix_ref[...], wo_ref[...])
    hn = _rms(h, nw_ref[...]).astype(BF16)
    first_row = lax.broadcasted_iota(jnp.int32, (pad, fc), 0) == 0

    def up(j):
        for part in range(2):
            u_ref[2 * j + part] = _dot(hn, win_ref[:, pl.ds(part * d_ff + j * fc, fc)])

    def shift(y, e, k):
        r = pltpu.roll(y, 1, axis=0)
        top = r[:pad, :]
        fixed = jnp.where(first_row, carry_ref[e, k], top)
        carry_ref[e, k] = top
        return jnp.concatenate([fixed, r[pad:, :]], axis=0)

    def conv(j, part):
        e = 2 * j + part
        cols = pl.ds(part * d_ff + j * fc, fc)
        u = u_ref[e]
        cw = cw_ref[:, cols]
        y = shift(u * cw[0:1, :], e, 0) + u * cw[1:2, :]
        return shift(y, e, 1) + u * cw[2:3, :] + cb_ref[:, cols]

    n_chunks = d_ff // fc
    out = h
    up(0)
    for j in range(n_chunks):
        if j + 1 < n_chunks:
            up(j + 1)
        a = conv(j, 0)
        g = conv(j, 1)
        grp, pos = divmod(j, FFN_DOWN_GROUP)
        act_ref[grp, :, pl.ds(pos * fc, fc)] = (g * _sigmoid(g) * a).astype(BF16)
        if pos + 1 == FFN_DOWN_GROUP or j + 1 == n_chunks:
            k = (pos + 1) * fc
            out = out + _dot(act_ref[grp, :, pl.ds(0, k)],
                             wout_ref[pl.ds(grp * FFN_DOWN_GROUP * fc, k), :])
    if final_norm:
        out = _rms(out, fnw_ref[...])
    out_ref[...] = out


def _ffn(h3, nw, w_in, conv_w, conv_b, w_out, mix3=None, wo=None, final_nw=None):
    B, S, D = h3.shape
    tm = TOK_BLOCK
    d_ff = w_out.shape[0]
    has_mix = mix3 is not None
    final_norm = final_nw is not None
    tok = pl.BlockSpec((None, tm, D), lambda b, s: (b, s, 0))
    args, specs = [h3], [tok]
    if has_mix:
        args += [mix3, wo]
        specs += [pl.BlockSpec((None, tm, mix3.shape[2]), lambda b, s: (b, s, 0)), _full(wo.shape)]
    args += [nw, w_in, conv_w, conv_b, w_out]
    specs += [_full(nw.shape), _full(w_in.shape), _full(conv_w.shape), _full(conv_b.shape),
              _full(w_out.shape)]
    if final_norm:
        args.append(final_nw)
        specs.append(_full(final_nw.shape))
    return pl.pallas_call(
        functools.partial(_ffn_kernel, has_mix=has_mix, final_norm=final_norm, d_ff=d_ff),
        grid=(B, S // tm),
        in_specs=specs,
        out_specs=tok,
        out_shape=jax.ShapeDtypeStruct((B, S, D), F32),
        scratch_shapes=[pltpu.VMEM((2 * (d_ff // FFN_CHUNK), tm, FFN_CHUNK), F32),
                        pltpu.VMEM((2 * (d_ff // FFN_CHUNK), 2, SUBLANES, FFN_CHUNK), F32),
                        pltpu.VMEM((pl.cdiv(d_ff // FFN_CHUNK, FFN_DOWN_GROUP), tm,
                                    FFN_DOWN_GROUP * FFN_CHUNK), BF16)],
        compiler_params=_params(2),
        name="ffn_mix" if has_mix else "ffn",
    )(*args)


def _rope(x, cos, sin_lo, sin_hi):
    half = LANES // 4
    return (x * cos + pltpu.roll(x, LANES - half, axis=1) * sin_lo
            + pltpu.roll(x, half, axis=1) * sin_hi)


def _qkv_proj_kernel(h_ref, kvnw_ref, anw_ref, wk_ref, wvt_ref, wq_ref, cos_ref, slo_ref, shi_ref,
                     q_ref, k_ref, vt_ref, *, q_scale):
    x = h_ref[...]
    ms = jnp.mean(x * x, axis=-1, keepdims=True)
    xn = x * lax.rsqrt(ms + EPS)
    kvn = (xn * kvnw_ref[...]).astype(BF16)
    an = (xn * anw_ref[...]).astype(BF16)
    k = _dot(kvn, wk_ref[...])
    q = _dot(an, wq_ref[...])
    vt = _dot_nt(wvt_ref[...], kvn)
    cos = cos_ref[...]
    slo = slo_ref[...]
    shi = shi_ref[...]
    for j in range(q.shape[1] // LANES):
        cols = slice(j * LANES, (j + 1) * LANES)
        q_ref[:, cols] = (_rope(q[:, cols], cos, slo, shi) * q_scale).astype(BF16)
        k_ref[:, cols] = _rope(k[:, cols], cos, slo, shi).astype(BF16)
    ones = jnp.ones((SUBLANES, vt.shape[1]), BF16)
    for h in range(vt_ref.shape[0]):
        vt_ref[h, :LANES, :] = vt[h * LANES:(h + 1) * LANES, :].astype(BF16)
        vt_ref[h, LANES:, :] = ones


def _qkv_proj(h3, kv_nw, a_nw, w_k, w_vt, w_q, cos, sin_lo, sin_hi, q_scale):
    B, S, D = h3.shape
    tm = TOK_BLOCK
    n_qk = w_q.shape[1]
    H = w_vt.shape[0] // LANES
    tok = lambda n: pl.BlockSpec((None, tm, n), lambda b, s: (b, s, 0))
    tab = pl.BlockSpec((tm, LANES), lambda b, s: (s, 0))
    return pl.pallas_call(
        functools.partial(_qkv_proj_kernel, q_scale=q_scale),
        grid=(B, S // tm),
        in_specs=[tok(D), _full(kv_nw.shape), _full(a_nw.shape), _full(w_k.shape), _full(w_vt.shape),
                  _full(w_q.shape), tab, tab, tab],
        out_specs=[tok(n_qk), tok(n_qk),
                   pl.BlockSpec((None, H, LANES + SUBLANES, tm), lambda b, s: (b, 0, 0, s))],
        out_shape=[jax.ShapeDtypeStruct((B, S, n_qk), BF16),
                   jax.ShapeDtypeStruct((B, S, n_qk), BF16),
                   jax.ShapeDtypeStruct((B, H, LANES + SUBLANES, S), BF16)],
        compiler_params=_params(2),
        name="qkv_proj",
    )(h3, kv_nw, a_nw, w_k, w_vt, w_q, cos, sin_lo, sin_hi)


def _diff_attn_kernel(lam_ref, q_ref, k_ref, vt_ref, swt_ref, o_ref,
                      s0_scr, s1_scr, p0_scr, p1_scr, a0_scr, a1_scr, m_scr, acc_scr, *, lam_init):
    bq = q_ref.shape[0]
    bk = bq
    hd = LANES // 2
    i = pl.program_id(2)
    s_bufs = (s0_scr, s1_scr)
    p_bufs = (p0_scr, p1_scr)
    a_bufs = (a0_scr, a1_scr)

    q = q_ref[...]
    lane = lax.broadcasted_iota(jnp.int32, q.shape, 1)
    zero = jnp.zeros_like(q)
    qs = jnp.concatenate([jnp.where(lane < hd, q, zero), jnp.where(lane >= hd, q, zero)], axis=0)

    m_scr[...] = jnp.full_like(m_scr, -jnp.inf)
    acc_scr[...] = jnp.zeros_like(acc_scr)

    def qk(n, slot):
        rows = pl.ds(pl.multiple_of(n * bk, bk), bk)
        s_bufs[slot][...] = _dot_nt(k_ref[rows, :], qs)

    def softmax(slot, masked):
        s = s_bufs[slot][...]
        if masked:
            kk = lax.broadcasted_iota(jnp.int32, (bk, bq), 0)
            qq = lax.broadcasted_iota(jnp.int32, (bk, bq), 1)
            keep = kk <= qq
            s = jnp.where(jnp.concatenate([keep, keep], axis=1), s, -jnp.inf)
        m_prev = m_scr[...]
        m_new = jnp.maximum(m_prev, jnp.max(s, axis=0, keepdims=True))
        a_bufs[slot][...] = jnp.exp2(m_prev - m_new)
        p_bufs[slot][...] = jnp.exp2(s - m_new).astype(BF16)
        m_scr[...] = m_new

    def pv(n, slot):
        cols = pl.ds(pl.multiple_of(n * bk, bk), bk)
        acc_scr[...] = a_bufs[slot][...] * acc_scr[...] + _dot(vt_ref[:, cols], p_bufs[slot][...])

    def stage(n, slot):
        pv(n - 1, 1 - slot)
        qk(n + 1, 1 - slot)
        softmax(slot, False)

    qk(0, 0)

    @pl.when(i > 0)
    def _():
        qk(1, 1)
        softmax(0, False)

    def body(n, carry):
        lax.cond(n % 2 == 1, lambda: stage(n, 1), lambda: stage(n, 0))
        return carry

    lax.fori_loop(1, i, body, 0)

    @pl.when(i % 2 == 1)
    def _():
        pv(i - 1, 0)
        softmax(1, True)
        pv(i, 1)

    @pl.when(i % 2 == 0)
    def _():
        @pl.when(i > 0)
        def _():
            pv(i - 1, 1)

        softmax(0, True)
        pv(i, 0)

    lp = lam_ref[...]
    lam = (jnp.exp(jnp.sum(lp[0:1, :] * lp[1:2, :], axis=-1, keepdims=True))
           - jnp.exp(jnp.sum(lp[2:3, :] * lp[3:4, :], axis=-1, keepdims=True)) + lam_init)
    acc = acc_scr[...]
    o_all = acc[:LANES, :] / acc[LANES:LANES + 1, :]
    ot = o_all[:, :bq] - lam * o_all[:, bq:]
    ms = jnp.mean(ot * ot, axis=0, keepdims=True)
    ot = ot * lax.rsqrt(ms + EPS) * swt_ref[...] * (1.0 - lam_init)
    o_ref[...] = ot.T.astype(BF16)


def _diff_attn(q3, k3, vt4, lam_p, subln_col, lam_init):
    B, S, n_qk = q3.shape
    bq = ATTN_BLOCK
    H = DIFF_HEADS
    return pl.pallas_call(
        functools.partial(_diff_attn_kernel, lam_init=lam_init),
        grid=(B, H, S // bq),
        in_specs=[
            _full(lam_p.shape),
            pl.BlockSpec((None, bq, LANES), lambda b, h, i: (b, i, h)),
            pl.BlockSpec((None, S, LANES), lambda b, h, i: (b, 0, h)),
            pl.BlockSpec((None, None, LANES + SUBLANES, S), lambda b, h, i: (b, h, 0, 0)),
            _full(subln_col.shape),
        ],
        out_specs=pl.BlockSpec((None, bq, LANES), lambda b, h, i: (b, i, h)),
        out_shape=jax.ShapeDtypeStruct((B, S, H * LANES), BF16),
        scratch_shapes=[
            pltpu.VMEM((bq, 2 * bq), F32),
            pltpu.VMEM((bq, 2 * bq), F32),
            pltpu.VMEM((bq, 2 * bq), BF16),
            pltpu.VMEM((bq, 2 * bq), BF16),
            pltpu.VMEM((1, 2 * bq), F32),
            pltpu.VMEM((1, 2 * bq), F32),
            pltpu.VMEM((1, 2 * bq), F32),
            pltpu.VMEM((LANES + SUBLANES, 2 * bq), F32),
        ],
        compiler_params=_params(3),
        name="diff_attn",
    )(lam_p, q3, k3, vt4, subln_col)


def _rope_tables(seq, hd):
    pos = jnp.arange(seq, dtype=F32)
    inv = ROPE_THETA ** (-jnp.arange(0, hd, 2, dtype=F32) / hd)
    f = pos[:, None] * inv[None, :]
    cos_h, sin_h = jnp.cos(f), jnp.sin(f)
    zeros = jnp.zeros_like(sin_h)
    cos = jnp.concatenate([cos_h, cos_h] * 2, axis=-1)
    sin_lo = jnp.concatenate([-sin_h, zeros] * 2, axis=-1)
    sin_hi = jnp.concatenate([zeros, sin_h] * 2, axis=-1)
    return cos, sin_lo, sin_hi


def kernel(x, attn_norm_w, ffn_norm_w, gla_w_qkvg, gla_w_gk1, gla_w_gk2, gla_b_gk, gla_onorm_w, gla_w_o, kv_norm_w, w_kv, diff_w_q, diff_lambda, diff_subln_w, diff_w_o, ffn_w_in, ffn_conv_w, ffn_conv_b, ffn_w_out, final_norm_w):
    B, S, D = x.shape
    depth = attn_norm_w.shape[0]
    n_a = gla_w_qkvg.shape[0]
    dk = gla_w_gk2.shape[2]
    dv = gla_w_o.shape[1]
    n_qk = diff_w_q.shape[2]
    hd = n_qk // (2 * DIFF_HEADS)
    assert hd * 2 == LANES and (w_kv.shape[1] - n_qk) == DIFF_HEADS * LANES
    assert S % TOK_BLOCK == 0 and S % ATTN_BLOCK == 0 and TOK_BLOCK % GLA_CHUNK == 0
    assert ffn_w_out.shape[1] % FFN_CHUNK == 0 and GLA_GATE_RANK <= LANES

    row = lambda v: v.reshape(1, -1)
    cos, sin_lo, sin_hi = _rope_tables(S, hd)

    h = x
    for l in range(depth):
        final_nw = row(final_norm_w) if l == depth - 1 else None
        ffn_args = (row(ffn_norm_w[l]), ffn_w_in[l].astype(BF16), ffn_conv_w[l], row(ffn_conv_b[l]),
                    ffn_w_out[l].astype(BF16))
        if l < n_a:
            w1_pad = jnp.pad(gla_w_gk1[l], ((0, 0), (0, LANES - GLA_GATE_RANK)))
            w_cat = jnp.concatenate([gla_w_qkvg[l], w1_pad], axis=1).astype(BF16)
            w2_pad = jnp.pad(gla_w_gk2[l], ((0, LANES - GLA_GATE_RANK), (0, 0))).astype(BF16)
            qkvg, gk = _gla_proj(h.reshape(B * S, D), row(attn_norm_w[l]), w_cat, w2_pad,
                                 row(gla_b_gk[l]))
            h = _gla_core(qkvg.reshape(B, S, -1), gk.reshape(B, S, dk), h, row(gla_onorm_w[l]),
                          gla_w_o[l].astype(BF16), dk, dv)
            h = _ffn(h, *ffn_args, final_nw=final_nw)
        else:
            j = l - n_a
            lam_init = 0.8 - 0.6 * math.exp(-0.3 * l)
            q3, k_new, vt_new = _qkv_proj(h, row(kv_norm_w), row(attn_norm_w[l]),
                                          w_kv[:, :n_qk].astype(BF16), w_kv[:, n_qk:].T.astype(BF16),
                                          diff_w_q[j].astype(BF16), cos, sin_lo, sin_hi,
                                          float(hd) ** -0.5 * LOG2E)
            if l == n_a:
                k3, vt4 = k_new, vt_new
            o3 = _diff_attn(q3, k3, vt4, diff_lambda[j], diff_subln_w[j].reshape(-1, 1), lam_init)
            h = _ffn(h, *ffn_args, mix3=o3, wo=diff_w_o[j].astype(BF16), final_nw=final_nw)
    return h
```

```python
import functools
import math

import jax
import jax.numpy as jnp
from jax import lax
from jax.experimental import pallas as pl
from jax.experimental.pallas import tpu as pltpu

F32 = jnp.float32
BF16 = jnp.bfloat16

EPS = 1e-6
ROPE_THETA = 10000.0
GLA_HEADS = 4
GLA_CHUNK = 64
GLA_GATE_RANK = 16
GLA_GATE_NORM = 16.0
DIFF_HEADS = 8
CONV_WIDTH = 3

LANES = 128
SUBLANES = 8
VMEM_LIMIT = 56 * 1024 * 1024

TOK_BLOCK = 512
FFN_CHUNK = 256
FFN_DOWN_GROUP = 4
ATTN_BLOCK = 512
LOG2E = math.log2(math.e)


def _params(n_axes):
    return pltpu.CompilerParams(
        dimension_semantics=("arbitrary",) * n_axes, vmem_limit_bytes=VMEM_LIMIT)


def _full(shape):
    nd = len(shape)
    return pl.BlockSpec(shape, lambda *_: (0,) * nd)


def _rms(x, w):
    ms = jnp.mean(x * x, axis=-1, keepdims=True)
    return x * lax.rsqrt(ms + EPS) * w


def _sigmoid(x):
    return 1.0 / (1.0 + jnp.exp2(x * (-LOG2E)))


def _dot(a, b):
    return jnp.dot(a, b, preferred_element_type=F32)


def _dot_nt(a, b):
    return lax.dot_general(a, b, (((1,), (1,)), ((), ())), preferred_element_type=F32)


def _dot_tn(a, b):
    return lax.dot_general(a, b, (((0,), (0,)), ((), ())), preferred_element_type=F32)


def _gla_proj_kernel(x_ref, nw_ref, w_ref, w2_ref, b_ref, qkvg_ref, gk_ref, *, n_main):
    hn = _rms(x_ref[...], nw_ref[...]).astype(BF16)
    proj = _dot(hn, w_ref[...])
    qkvg_ref[...] = proj[:, :n_main].astype(BF16)
    low = proj[:, n_main:].astype(BF16)
    z = _dot(low, w2_ref[...]) + b_ref[...]
    gk_ref[...] = (jnp.minimum(z, 0.0) - jnp.log(1.0 + jnp.exp(-jnp.abs(z)))) * (1.0 / GLA_GATE_NORM)


def _gla_proj(x2, nw, w_cat, w2_pad, b_gk):
    T, D = x2.shape
    n_main = w_cat.shape[1] - LANES
    dk = w2_pad.shape[1]
    tm = TOK_BLOCK
    return pl.pallas_call(
        functools.partial(_gla_proj_kernel, n_main=n_main),
        grid=(T // tm,),
        in_specs=[
            pl.BlockSpec((tm, D), lambda i: (i, 0)),
            _full(nw.shape), _full(w_cat.shape), _full(w2_pad.shape), _full(b_gk.shape),
        ],
        out_specs=[
            pl.BlockSpec((tm, n_main), lambda i: (i, 0)),
            pl.BlockSpec((tm, dk), lambda i: (i, 0)),
        ],
        out_shape=[
            jax.ShapeDtypeStruct((T, n_main), BF16),
            jax.ShapeDtypeStruct((T, dk), F32),
        ],
        compiler_params=_params(1),
        name="gla_proj",
    )(x2, nw, w_cat, w2_pad, b_gk)


def _cumsum_rows(g):
    n = g.shape[0]
    ridx = lax.broadcasted_iota(jnp.int32, g.shape, 0)
    b = g
    s = 1
    while s < n:
        b = b + jnp.where(ridx >= s, pltpu.roll(b, s, axis=0), 0.0)
        s *= 2
    return b


def _gla_core_kernel(qkvg_ref, gk_ref, x_ref, onw_ref, wo_ref, out_ref, state_ref, o_scr,
                     *, dk, dv):
    C = GLA_CHUNK
    H = GLA_HEADS
    hk = dk // H
    hv = dv // H
    tb = x_ref.shape[0]

    @pl.when(pl.program_id(1) == 0)
    def _():
        state_ref[...] = jnp.zeros_like(state_ref)

    row = lax.broadcasted_iota(jnp.int32, (C, C), 0)
    col = lax.broadcasted_iota(jnp.int32, (C, C), 1)
    causal = col <= row
    qscale = hk ** -0.5

    for c in range(tb // C):
        r = pl.ds(c * C, C)
        b_all = _cumsum_rows(gk_ref[r, :])
        for h in range(H):
            q = qkvg_ref[r, h * hk:(h + 1) * hk].astype(F32)
            k = qkvg_ref[r, dk + h * hk:dk + (h + 1) * hk].astype(F32)
            v = qkvg_ref[r, 2 * dk + h * hv:2 * dk + (h + 1) * hv]
            b = b_all[:, h * hk:(h + 1) * hk]
            b_last = b[C - 1:C, :]
            q_in = (q * qscale * jnp.exp(b)).astype(BF16)
            k_in = (k * jnp.exp(-b)).astype(BF16)
            k_end = (k * jnp.exp(b_last - b)).astype(BF16)
            scores = jnp.where(causal, _dot_nt(q_in, k_in), 0.0)
            st = state_ref[h]
            o = _dot(scores.astype(BF16), v) + _dot_nt(q_in, st.astype(BF16))
            o_scr[r, h * hv:(h + 1) * hv] = o
            state_ref[h] = st * jnp.exp(b_last) + _dot_tn(v, k_end)

    onw = onw_ref[...]
    parts = []
    for h in range(H):
        parts.append(_rms(o_scr[:, h * hv:(h + 1) * hv], onw))
    on = jnp.concatenate(parts, axis=1)
    g = qkvg_ref[:, 2 * dk + dv:2 * dk + 2 * dv].astype(F32)
    gated = (on * (g * _sigmoid(g))).astype(BF16)
    out_ref[...] = x_ref[...] + _dot(gated, wo_ref[...])


def _gla_core(qkvg3, gk3, x3, onw, wo, dk, dv):
    B, S, D = x3.shape
    tb = TOK_BLOCK
    H = GLA_HEADS
    return pl.pallas_call(
        functools.partial(_gla_core_kernel, dk=dk, dv=dv),
        grid=(B, S // tb),
        in_specs=[
            pl.BlockSpec((None, tb, qkvg3.shape[2]), lambda b, s: (b, s, 0)),
            pl.BlockSpec((None, tb, dk), lambda b, s: (b, s, 0)),
            pl.BlockSpec((None, tb, D), lambda b, s: (b, s, 0)),
            _full(onw.shape), _full(wo.shape),
        ],
        out_specs=pl.BlockSpec((None, tb, D), lambda b, s: (b, s, 0)),
        out_shape=jax.ShapeDtypeStruct((B, S, D), F32),
        scratch_shapes=[
            pltpu.VMEM((H, dv // H, dk // H), F32),
            pltpu.VMEM((tb, dv), F32),
        ],
        compiler_params=_params(2),
        name="gla_core",
    )(qkvg3, gk3, x3, onw, wo)


def _ffn_kernel(*refs, has_mix, final_norm, d_ff):
    refs = list(refs)
    h_ref = refs.pop(0)
    if has_mix:
        mix_ref = refs.pop(0)
        wo_ref = refs.pop(0)
    nw_ref, win_ref, cw_ref, cb_ref, wout_ref = refs[:5]
    refs = refs[5:]
    if final_norm:
        fnw_ref = refs.pop(0)
    out_ref, u_ref, carry_ref, act_ref = refs

    tm = h_ref.shape[0]
    fc = FFN_CHUNK
    pad = SUBLANES

    @pl.when(pl.program_id(1) == 0)
    def _():
        carry_ref[...] = jnp.zeros_like(carry_ref)

    h = h_ref[...]
    if has_mix:
        mix = jnp.concatenate([mix_ref[i] for i in range(mix_ref.shape[0])], axis=1)
        h = h + _dot(mix, wo_ref[...])
    hn = _rms(h, nw_ref[...]).astype(BF16)
    first_row = lax.broadcasted_iota(jnp.int32, (pad, fc), 0) == 0

    def up(j):
        for part in range(2):
            u_ref[2 * j + part] = _dot(hn, win_ref[:, pl.ds(part * d_ff + j * fc, fc)])

    def shift(y, e, k):
        r = pltpu.roll(y, 1, axis=0)
        top = r[:pad, :]
        fixed = jnp.where(first_row, carry_ref[e, k], top)
        carry_ref[e, k] = top
        return jnp.concatenate([fixed, r[pad:, :]], axis=0)

    def conv(j, part):
        e = 2 * j + part
        cols = pl.ds(part * d_ff + j * fc, fc)
        u = u_ref[e]
        cw = cw_ref[:, cols]
        y = shift(u * cw[0:1, :], e, 0) + u * cw[1:2, :]
        return shift(y, e, 1) + u * cw[2:3, :] + cb_ref[:, cols]

    n_chunks = d_ff // fc
    out = h
    up(0)
    for j in range(n_chunks):
        if j + 1 < n_chunks:
            up(j + 1)
        a = conv(j, 0)
        g = conv(j, 1)
        grp, pos = divmod(j, FFN_DOWN_GROUP)
        act_ref[grp, :, pl.ds(pos * fc, fc)] = (g * _sigmoid(g) * a).astype(BF16)
        if pos + 1 == FFN_DOWN_GROUP or j + 1 == n_chunks:
            k = (pos + 1) * fc
            out = out + _dot(act_ref[grp, :, pl.ds(0, k)],
                             wout_ref[pl.ds(grp * FFN_DOWN_GROUP * fc, k), :])
    if final_norm:
        out = _rms(out, fnw_ref[...])
    out_ref[...] = out


def _ffn(h3, nw, w_in, conv_w, conv_b, w_out, mix3=None, wo=None, final_nw=None):
    B, S, D = h3.shape
    tm = TOK_BLOCK
    d_ff = w_out.shape[0]
    has_mix = mix3 is not None
    final_norm = final_nw is not None
    tok = pl.BlockSpec((None, tm, D), lambda b, s: (b, s, 0))
    args, specs = [h3], [tok]
    if has_mix:
        args += [mix3, wo]
        specs += [pl.BlockSpec((None, mix3.shape[1], tm, mix3.shape[3]), lambda b, s: (b, 0, s, 0)),
                  _full(wo.shape)]
    args += [nw, w_in, conv_w, conv_b, w_out]
    specs += [_full(nw.shape), _full(w_in.shape), _full(conv_w.shape), _full(conv_b.shape),
              _full(w_out.shape)]
    if final_norm:
        args.append(final_nw)
        specs.append(_full(final_nw.shape))
    return pl.pallas_call(
        functools.partial(_ffn_kernel, has_mix=has_mix, final_norm=final_norm, d_ff=d_ff),
        grid=(B, S // tm),
        in_specs=specs,
        out_specs=tok,
        out_shape=jax.ShapeDtypeStruct((B, S, D), F32),
        scratch_shapes=[pltpu.VMEM((2 * (d_ff // FFN_CHUNK), tm, FFN_CHUNK), F32),
                        pltpu.VMEM((2 * (d_ff // FFN_CHUNK), 2, SUBLANES, FFN_CHUNK), F32),
                        pltpu.VMEM((pl.cdiv(d_ff // FFN_CHUNK, FFN_DOWN_GROUP), tm,
                                    FFN_DOWN_GROUP * FFN_CHUNK), BF16)],
        compiler_params=_params(2),
        name="ffn_mix" if has_mix else "ffn",
    )(*args)


def _rope(x, cos, sin_lo, sin_hi):
    half = LANES // 4
    return (x * cos + pltpu.roll(x, LANES - half, axis=1) * sin_lo
            + pltpu.roll(x, half, axis=1) * sin_hi)


def _qkv_proj_kernel(h_ref, kvnw_ref, anw_ref, wk_ref, wvt_ref, wq_ref, cos_ref, slo_ref, shi_ref,
                     q_ref, k_ref, vt_ref, *, q_scale):
    x = h_ref[...]
    ms = jnp.mean(x * x, axis=-1, keepdims=True)
    xn = x * lax.rsqrt(ms + EPS)
    kvn = (xn * kvnw_ref[...]).astype(BF16)
    an = (xn * anw_ref[...]).astype(BF16)
    k = _dot(kvn, wk_ref[...])
    q = _dot(an, wq_ref[...])
    vt = _dot_nt(wvt_ref[...], kvn)
    cos = cos_ref[...]
    slo = slo_ref[...]
    shi = shi_ref[...]
    for j in range(q.shape[1] // LANES):
        cols = slice(j * LANES, (j + 1) * LANES)
        q_ref[j] = (_rope(q[:, cols], cos, slo, shi) * q_scale).astype(BF16)
        k_ref[j] = _rope(k[:, cols], cos, slo, shi).astype(BF16)
    ones = jnp.ones((SUBLANES, vt.shape[1]), BF16)
    for h in range(vt_ref.shape[0]):
        vt_ref[h, :LANES, :] = vt[h * LANES:(h + 1) * LANES, :].astype(BF16)
        vt_ref[h, LANES:, :] = ones


def _qkv_proj(h3, kv_nw, a_nw, w_k, w_vt, w_q, cos, sin_lo, sin_hi, q_scale):
    B, S, D = h3.shape
    tm = TOK_BLOCK
    n_qk = w_q.shape[1]
    H = w_vt.shape[0] // LANES
    tok = lambda n: pl.BlockSpec((None, tm, n), lambda b, s: (b, s, 0))
    tab = pl.BlockSpec((tm, LANES), lambda b, s: (s, 0))
    return pl.pallas_call(
        functools.partial(_qkv_proj_kernel, q_scale=q_scale),
        grid=(B, S // tm),
        in_specs=[tok(D), _full(kv_nw.shape), _full(a_nw.shape), _full(w_k.shape), _full(w_vt.shape),
                  _full(w_q.shape), tab, tab, tab],
        out_specs=[pl.BlockSpec((None, H, tm, LANES), lambda b, s: (b, 0, s, 0)),
                   pl.BlockSpec((None, H, tm, LANES), lambda b, s: (b, 0, s, 0)),
                   pl.BlockSpec((None, H, LANES + SUBLANES, tm), lambda b, s: (b, 0, 0, s))],
        out_shape=[jax.ShapeDtypeStruct((B, H, S, LANES), BF16),
                   jax.ShapeDtypeStruct((B, H, S, LANES), BF16),
                   jax.ShapeDtypeStruct((B, H, LANES + SUBLANES, S), BF16)],
        compiler_params=_params(2),
        name="qkv_proj",
    )(h3, kv_nw, a_nw, w_k, w_vt, w_q, cos, sin_lo, sin_hi)


def _diff_attn_kernel(lam_ref, q_ref, k_ref, vt_ref, swt_ref, o_ref,
                      s0_scr, s1_scr, p0_scr, p1_scr, a0_scr, a1_scr, m_scr, acc_scr, *, lam_init):
    bq = q_ref.shape[0]
    bk = bq
    hd = LANES // 2
    i = pl.program_id(2)
    s_bufs = (s0_scr, s1_scr)
    p_bufs = (p0_scr, p1_scr)
    a_bufs = (a0_scr, a1_scr)

    q = q_ref[...]
    lane = lax.broadcasted_iota(jnp.int32, q.shape, 1)
    zero = jnp.zeros_like(q)
    qs = jnp.concatenate([jnp.where(lane < hd, q, zero), jnp.where(lane >= hd, q, zero)], axis=0)

    m_scr[...] = jnp.full_like(m_scr, -jnp.inf)
    acc_scr[...] = jnp.zeros_like(acc_scr)

    def qk(n, slot):
        rows = pl.ds(pl.multiple_of(n * bk, bk), bk)
        s_bufs[slot][...] = _dot_nt(k_ref[rows, :], qs)

    def softmax(slot, masked):
        s = s_bufs[slot][...]
        if masked:
            kk = lax.broadcasted_iota(jnp.int32, (bk, bq), 0)
            qq = lax.broadcasted_iota(jnp.int32, (bk, bq), 1)
            keep = kk <= qq
            s = jnp.where(jnp.concatenate([keep, keep], axis=1), s, -jnp.inf)
        m_prev = m_scr[...]
        m_new = jnp.maximum(m_prev, jnp.max(s, axis=0, keepdims=True))
        a_bufs[slot][...] = jnp.exp2(m_prev - m_new)
        p_bufs[slot][...] = jnp.exp2(s - m_new).astype(BF16)
        m_scr[...] = m_new

    def pv(n, slot):
        cols = pl.ds(pl.multiple_of(n * bk, bk), bk)
        acc_scr[...] = a_bufs[slot][...] * acc_scr[...] + _dot(vt_ref[:, cols], p_bufs[slot][...])

    def stage(n, slot):
        pv(n - 1, 1 - slot)
        qk(n + 1, 1 - slot)
        softmax(slot, False)

    def head(more):
        qk(0, 0)
        if more:
            qk(1, 1)
            softmax(0, False)

    lax.cond(i > 0, lambda: head(True), lambda: head(False))

    def body(n, carry):
        lax.cond(n % 2 == 1, lambda: stage(n, 1), lambda: stage(n, 0))
        return carry

    lax.fori_loop(1, i, body, 0)

    def tail(slot, has_prev):
        if has_prev:
            pv(i - 1, 1 - slot)
        softmax(slot, True)
        pv(i, slot)
        lp = lam_ref[...]
        lam = (jnp.exp(jnp.sum(lp[0:1, :] * lp[1:2, :], axis=-1, keepdims=True))
               - jnp.exp(jnp.sum(lp[2:3, :] * lp[3:4, :], axis=-1, keepdims=True)) + lam_init)
        acc = acc_scr[...]
        o_all = acc[:LANES, :] / acc[LANES:LANES + 1, :]
        ot = o_all[:, :bq] - lam * o_all[:, bq:]
        ms = jnp.mean(ot * ot, axis=0, keepdims=True)
        ot = ot * lax.rsqrt(ms + EPS) * swt_ref[...] * (1.0 - lam_init)
        o_ref[...] = ot.T.astype(BF16)

    lax.cond(i % 2 == 1, lambda: tail(1, True),
             lambda: lax.cond(i > 0, lambda: tail(0, True), lambda: tail(0, False)))


def _diff_attn(q3, k3, vt4, lam_p, subln_col, lam_init):
    B, _, S, _ = q3.shape
    bq = ATTN_BLOCK
    H = DIFF_HEADS
    return pl.pallas_call(
        functools.partial(_diff_attn_kernel, lam_init=lam_init),
        grid=(B, H, S // bq),
        in_specs=[
            _full(lam_p.shape),
            pl.BlockSpec((None, None, bq, LANES), lambda b, h, i: (b, h, i, 0)),
            pl.BlockSpec((None, None, S, LANES), lambda b, h, i: (b, h, 0, 0)),
            pl.BlockSpec((None, None, LANES + SUBLANES, S), lambda b, h, i: (b, h, 0, 0)),
            _full(subln_col.shape),
        ],
        out_specs=pl.BlockSpec((None, None, bq, LANES), lambda b, h, i: (b, h, i, 0)),
        out_shape=jax.ShapeDtypeStruct((B, H, S, LANES), BF16),
        scratch_shapes=[
            pltpu.VMEM((bq, 2 * bq), F32),
            pltpu.VMEM((bq, 2 * bq), F32),
            pltpu.VMEM((bq, 2 * bq), BF16),
            pltpu.VMEM((bq, 2 * bq), BF16),
            pltpu.VMEM((1, 2 * bq), F32),
            pltpu.VMEM((1, 2 * bq), F32),
            pltpu.VMEM((1, 2 * bq), F32),
            pltpu.VMEM((LANES + SUBLANES, 2 * bq), F32),
        ],
        compiler_params=_params(3),
        name="diff_attn",
    )(lam_p, q3, k3, vt4, subln_col)


def _rope_tables(seq, hd):
    pos = jnp.arange(seq, dtype=F32)
    inv = ROPE_THETA ** (-jnp.arange(0, hd, 2, dtype=F32) / hd)
    f = pos[:, None] * inv[None, :]
    cos_h, sin_h = jnp.cos(f), jnp.sin(f)
    zeros = jnp.zeros_like(sin_h)
    cos = jnp.concatenate([cos_h, cos_h] * 2, axis=-1)
    sin_lo = jnp.concatenate([-sin_h, zeros] * 2, axis=-1)
    sin_hi = jnp.concatenate([zeros, sin_h] * 2, axis=-1)
    return cos, sin_lo, sin_hi


def kernel(x, attn_norm_w, ffn_norm_w, gla_w_qkvg, gla_w_gk1, gla_w_gk2, gla_b_gk, gla_onorm_w, gla_w_o, kv_norm_w, w_kv, diff_w_q, diff_lambda, diff_subln_w, diff_w_o, ffn_w_in, ffn_conv_w, ffn_conv_b, ffn_w_out, final_norm_w):
    B, S, D = x.shape
    depth = attn_norm_w.shape[0]
    n_a = gla_w_qkvg.shape[0]
    dk = gla_w_gk2.shape[2]
    dv = gla_w_o.shape[1]
    n_qk = diff_w_q.shape[2]
    hd = n_qk // (2 * DIFF_HEADS)
    assert hd * 2 == LANES and (w_kv.shape[1] - n_qk) == DIFF_HEADS * LANES
    assert S % TOK_BLOCK == 0 and S % ATTN_BLOCK == 0 and TOK_BLOCK % GLA_CHUNK == 0
    assert ffn_w_out.shape[1] % FFN_CHUNK == 0 and GLA_GATE_RANK <= LANES

    row = lambda v: v.reshape(1, -1)
    cos, sin_lo, sin_hi = _rope_tables(S, hd)

    h = x
    for l in range(depth):
        final_nw = row(final_norm_w) if l == depth - 1 else None
        ffn_args = (row(ffn_norm_w[l]), ffn_w_in[l].astype(BF16), ffn_conv_w[l], row(ffn_conv_b[l]),
                    ffn_w_out[l].astype(BF16))
        if l < n_a:
            w1_pad = jnp.pad(gla_w_gk1[l], ((0, 0), (0, LANES - GLA_GATE_RANK)))
            w_cat = jnp.concatenate([gla_w_qkvg[l], w1_pad], axis=1).astype(BF16)
            w2_pad = jnp.pad(gla_w_gk2[l], ((0, LANES - GLA_GATE_RANK), (0, 0))).astype(BF16)
            qkvg, gk = _gla_proj(h.reshape(B * S, D), row(attn_norm_w[l]), w_cat, w2_pad,
                                 row(gla_b_gk[l]))
            h = _gla_core(qkvg.reshape(B, S, -1), gk.reshape(B, S, dk), h, row(gla_onorm_w[l]),
                          gla_w_o[l].astype(BF16), dk, dv)
            h = _ffn(h, *ffn_args, final_nw=final_nw)
        else:
            j = l - n_a
            lam_init = 0.8 - 0.6 * math.exp(-0.3 * l)
            q3, k_new, vt_new = _qkv_proj(h, row(kv_norm_w), row(attn_norm_w[l]),
                                          w_kv[:, :n_qk].astype(BF16), w_kv[:, n_qk:].T.astype(BF16),
                                          diff_w_q[j].astype(BF16), cos, sin_lo, sin_hi,
                                          float(hd) ** -0.5 * LOG2E)
            if l == n_a:
                k3, vt4 = k_new, vt_new
            o3 = _diff_attn(q3, k3, vt4, diff_lambda[j], diff_subln_w[j].reshape(-1, 1), lam_init)
            h = _ffn(h, *ffn_args, mix3=o3, wo=diff_w_o[j].astype(BF16), final_nw=final_nw)
    return h
```

```python
import functools
import math

import jax
import jax.numpy as jnp
from jax import lax
from jax.experimental import pallas as pl
from jax.experimental.pallas import tpu as pltpu

F32 = jnp.float32
BF16 = jnp.bfloat16

EPS = 1e-6
ROPE_THETA = 10000.0
GLA_HEADS = 4
GLA_CHUNK = 64
GLA_GATE_RANK = 16
GLA_GATE_NORM = 16.0
DIFF_HEADS = 8
CONV_WIDTH = 3

LANES = 128
SUBLANES = 8
VMEM_LIMIT = 56 * 1024 * 1024

TOK_BLOCK = 512
FFN_CHUNK = 256
FFN_DOWN_GROUP = 4
ATTN_BLOCK = 512
LOG2E = math.log2(math.e)


def _params(n_axes):
    return pltpu.CompilerParams(
        dimension_semantics=("arbitrary",) * n_axes, vmem_limit_bytes=VMEM_LIMIT)


def _full(shape):
    nd = len(shape)
    return pl.BlockSpec(shape, lambda *_: (0,) * nd)


def _rms(x, w):
    ms = jnp.mean(x * x, axis=-1, keepdims=True)
    return x * lax.rsqrt(ms + EPS) * w


def _sigmoid(x):
    return 1.0 / (1.0 + jnp.exp2(x * (-LOG2E)))


def _dot(a, b):
    return jnp.dot(a, b, preferred_element_type=F32)


def _dot_nt(a, b):
    return lax.dot_general(a, b, (((1,), (1,)), ((), ())), preferred_element_type=F32)


def _dot_tn(a, b):
    return lax.dot_general(a, b, (((0,), (0,)), ((), ())), preferred_element_type=F32)


def _gla_proj_kernel(x_ref, nw_ref, w_ref, w2_ref, b_ref, qkvg_ref, gk_ref, *, n_main):
    hn = _rms(x_ref[...], nw_ref[...]).astype(BF16)
    proj = _dot(hn, w_ref[...])
    qkvg_ref[...] = proj[:, :n_main].astype(BF16)
    low = proj[:, n_main:].astype(BF16)
    z = _dot(low, w2_ref[...]) + b_ref[...]
    gk_ref[...] = (jnp.minimum(z, 0.0) - jnp.log(1.0 + jnp.exp(-jnp.abs(z)))) * (1.0 / GLA_GATE_NORM)


def _gla_proj(x2, nw, w_cat, w2_pad, b_gk):
    T, D = x2.shape
    n_main = w_cat.shape[1] - LANES
    dk = w2_pad.shape[1]
    tm = TOK_BLOCK
    return pl.pallas_call(
        functools.partial(_gla_proj_kernel, n_main=n_main),
        grid=(T // tm,),
        in_specs=[
            pl.BlockSpec((tm, D), lambda i: (i, 0)),
            _full(nw.shape), _full(w_cat.shape), _full(w2_pad.shape), _full(b_gk.shape),
        ],
        out_specs=[
            pl.BlockSpec((tm, n_main), lambda i: (i, 0)),
            pl.BlockSpec((tm, dk), lambda i: (i, 0)),
        ],
        out_shape=[
            jax.ShapeDtypeStruct((T, n_main), BF16),
            jax.ShapeDtypeStruct((T, dk), F32),
        ],
        compiler_params=_params(1),
        name="gla_proj",
    )(x2, nw, w_cat, w2_pad, b_gk)


def _cumsum_rows(g):
    n = g.shape[0]
    ridx = lax.broadcasted_iota(jnp.int32, g.shape, 0)
    b = g
    s = 1
    while s < n:
        b = b + jnp.where(ridx >= s, pltpu.roll(b, s, axis=0), 0.0)
        s *= 2
    return b


def _gla_core_kernel(qkvg_ref, gk_ref, x_ref, onw_ref, wo_ref, out_ref, state_ref, o_scr,
                     *, dk, dv):
    C = GLA_CHUNK
    H = GLA_HEADS
    hk = dk // H
    hv = dv // H
    tb = x_ref.shape[0]

    @pl.when(pl.program_id(1) == 0)
    def _():
        state_ref[...] = jnp.zeros_like(state_ref)

    row = lax.broadcasted_iota(jnp.int32, (C, C), 0)
    col = lax.broadcasted_iota(jnp.int32, (C, C), 1)
    causal = col <= row
    qscale = hk ** -0.5

    for c in range(tb // C):
        r = pl.ds(c * C, C)
        b_all = _cumsum_rows(gk_ref[r, :])
        for h in range(H):
            q = qkvg_ref[r, h * hk:(h + 1) * hk].astype(F32)
            k = qkvg_ref[r, dk + h * hk:dk + (h + 1) * hk].astype(F32)
            v = qkvg_ref[r, 2 * dk + h * hv:2 * dk + (h + 1) * hv]
            b = b_all[:, h * hk:(h + 1) * hk]
            b_last = b[C - 1:C, :]
            q_in = (q * qscale * jnp.exp(b)).astype(BF16)
            k_in = (k * jnp.exp(-b)).astype(BF16)
            k_end = (k * jnp.exp(b_last - b)).astype(BF16)
            scores = jnp.where(causal, _dot_nt(q_in, k_in), 0.0)
            st = state_ref[h]
            o = _dot(scores.astype(BF16), v) + _dot_nt(q_in, st.astype(BF16))
            o_scr[r, h * hv:(h + 1) * hv] = o
            state_ref[h] = st * jnp.exp(b_last) + _dot_tn(v, k_end)

    onw = onw_ref[...]
    parts = []
    for h in range(H):
        parts.append(_rms(o_scr[:, h * hv:(h + 1) * hv], onw))
    on = jnp.concatenate(parts, axis=1)
    g = qkvg_ref[:, 2 * dk + dv:2 * dk + 2 * dv].astype(F32)
    gated = (on * (g * _sigmoid(g))).astype(BF16)
    out_ref[...] = x_ref[...] + _dot(gated, wo_ref[...])


def _gla_core(qkvg3, gk3, x3, onw, wo, dk, dv):
    B, S, D = x3.shape
    tb = TOK_BLOCK
    H = GLA_HEADS
    return pl.pallas_call(
        functools.partial(_gla_core_kernel, dk=dk, dv=dv),
        grid=(B, S // tb),
        in_specs=[
            pl.BlockSpec((None, tb, qkvg3.shape[2]), lambda b, s: (b, s, 0)),
            pl.BlockSpec((None, tb, dk), lambda b, s: (b, s, 0)),
            pl.BlockSpec((None, tb, D), lambda b, s: (b, s, 0)),
            _full(onw.shape), _full(wo.shape),
        ],
        out_specs=pl.BlockSpec((None, tb, D), lambda b, s: (b, s, 0)),
        out_shape=jax.ShapeDtypeStruct((B, S, D), F32),
        scratch_shapes=[
            pltpu.VMEM((H, dv // H, dk // H), F32),
            pltpu.VMEM((tb, dv), F32),
        ],
        compiler_params=_params(2),
        name="gla_core",
    )(qkvg3, gk3, x3, onw, wo)


def _ffn_kernel(*refs, has_mix, final_norm, d_ff):
    refs = list(refs)
    h_ref = refs.pop(0)
    if has_mix:
        mix_ref = refs.pop(0)
        wo_ref = refs.pop(0)
    nw_ref, win_ref, cw_ref, cb_ref, wout_ref = refs[:5]
    refs = refs[5:]
    if final_norm:
        fnw_ref = refs.pop(0)
    out_ref, u_ref, carry_ref, act_ref = refs

    tm = h_ref.shape[0]
    fc = FFN_CHUNK
    pad = SUBLANES

    @pl.when(pl.program_id(1) == 0)
    def _():
        carry_ref[...] = jnp.zeros_like(carry_ref)

    h = h_ref[...]
    if has_mix:
        mix = jnp.concatenate([mix_ref[i] for i in range(mix_ref.shape[0])], axis=1)
        h = h + _dot(mix, wo_ref[...])
    hn = _rms(h, nw_ref[...]).astype(BF16)
    first_row = lax.broadcasted_iota(jnp.int32, (pad, fc), 0) == 0

    def up(j):
        for part in range(2):
            u_ref[2 * j + part] = _dot(hn, win_ref[:, pl.ds(part * d_ff + j * fc, fc)])

    def shift(y, e, k):
        r = pltpu.roll(y, 1, axis=0)
        top = r[:pad, :]
        fixed = jnp.where(first_row, carry_ref[e, k], top)
        carry_ref[e, k] = top
        return jnp.concatenate([fixed, r[pad:, :]], axis=0)

    def conv(j, part):
        e = 2 * j + part
        cols = pl.ds(part * d_ff + j * fc, fc)
        u = u_ref[e]
        cw = cw_ref[:, cols]
        y = shift(u * cw[0:1, :], e, 0) + u * cw[1:2, :]
        return shift(y, e, 1) + u * cw[2:3, :] + cb_ref[:, cols]

    n_chunks = d_ff // fc
    out = h
    up(0)
    for j in range(n_chunks):
        if j + 1 < n_chunks:
            up(j + 1)
        a = conv(j, 0)
        g = conv(j, 1)
        grp, pos = divmod(j, FFN_DOWN_GROUP)
        act_ref[grp, :, pl.ds(pos * fc, fc)] = (g * _sigmoid(g) * a).astype(BF16)
        if pos + 1 == FFN_DOWN_GROUP or j + 1 == n_chunks:
            k = (pos + 1) * fc
            out = out + _dot(act_ref[grp, :, pl.ds(0, k)],
                             wout_ref[pl.ds(grp * FFN_DOWN_GROUP * fc, k), :])
    if final_norm:
        out = _rms(out, fnw_ref[...])
    out_ref[...] = out


def _ffn(h3, nw, w_in, conv_w, conv_b, w_out, mix3=None, wo=None, final_nw=None):
    B, S, D = h3.shape
    tm = TOK_BLOCK
    d_ff = w_out.shape[0]
    has_mix = mix3 is not None
    final_norm = final_nw is not None
    tok = pl.BlockSpec((None, tm, D), lambda b, s: (b, s, 0))
    args, specs = [h3], [tok]
    if has_mix:
        args += [mix3, wo]
        specs += [pl.BlockSpec((None, mix3.shape[1], tm, mix3.shape[3]), lambda b, s: (b, 0, s, 0)),
                  _full(wo.shape)]
    args += [nw, w_in, conv_w, conv_b, w_out]
    specs += [_full(nw.shape), _full(w_in.shape), _full(conv_w.shape), _full(conv_b.shape),
              _full(w_out.shape)]
    if final_norm:
        args.append(final_nw)
        specs.append(_full(final_nw.shape))
    return pl.pallas_call(
        functools.partial(_ffn_kernel, has_mix=has_mix, final_norm=final_norm, d_ff=d_ff),
        grid=(B, S // tm),
        in_specs=specs,
        out_specs=tok,
        out_shape=jax.ShapeDtypeStruct((B, S, D), F32),
        scratch_shapes=[pltpu.VMEM((2 * (d_ff // FFN_CHUNK), tm, FFN_CHUNK), F32),
                        pltpu.VMEM((2 * (d_ff // FFN_CHUNK), 2, SUBLANES, FFN_CHUNK), F32),
                        pltpu.VMEM((pl.cdiv(d_ff // FFN_CHUNK, FFN_DOWN_GROUP), tm,
                                    FFN_DOWN_GROUP * FFN_CHUNK), BF16)],
        compiler_params=_params(2),
        name="ffn_mix" if has_mix else "ffn",
    )(*args)


def _rope(x, cos, sin_lo, sin_hi):
    half = LANES // 4
    return (x * cos + pltpu.roll(x, LANES - half, axis=1) * sin_lo
            + pltpu.roll(x, half, axis=1) * sin_hi)


def _qkv_proj_kernel(h_ref, kvnw_ref, anw_ref, wk_ref, wvt_ref, wq_ref, cos_ref, slo_ref, shi_ref,
                     q_ref, k_ref, vt_ref, *, q_scale):
    x = h_ref[...]
    ms = jnp.mean(x * x, axis=-1, keepdims=True)
    xn = x * lax.rsqrt(ms + EPS)
    kvn = (xn * kvnw_ref[...]).astype(BF16)
    an = (xn * anw_ref[...]).astype(BF16)
    k = _dot(kvn, wk_ref[...])
    q = _dot(an, wq_ref[...])
    vt = _dot_nt(wvt_ref[...], kvn)
    cos = cos_ref[...]
    slo = slo_ref[...]
    shi = shi_ref[...]
    for j in range(q.shape[1] // LANES):
        cols = slice(j * LANES, (j + 1) * LANES)
        q_ref[j] = (_rope(q[:, cols], cos, slo, shi) * q_scale).astype(BF16)
        k_ref[j] = _rope(k[:, cols], cos, slo, shi).astype(BF16)
    ones = jnp.ones((SUBLANES, vt.shape[1]), BF16)
    for h in range(vt_ref.shape[0]):
        vt_ref[h, :LANES, :] = vt[h * LANES:(h + 1) * LANES, :].astype(BF16)
        vt_ref[h, LANES:, :] = ones


def _qkv_proj(h3, kv_nw, a_nw, w_k, w_vt, w_q, cos, sin_lo, sin_hi, q_scale):
    B, S, D = h3.shape
    tm = TOK_BLOCK
    n_qk = w_q.shape[1]
    H = w_vt.shape[0] // LANES
    tok = lambda n: pl.BlockSpec((None, tm, n), lambda b, s: (b, s, 0))
    tab = pl.BlockSpec((tm, LANES), lambda b, s: (s, 0))
    return pl.pallas_call(
        functools.partial(_qkv_proj_kernel, q_scale=q_scale),
        grid=(B, S // tm),
        in_specs=[tok(D), _full(kv_nw.shape), _full(a_nw.shape), _full(w_k.shape), _full(w_vt.shape),
                  _full(w_q.shape), tab, tab, tab],
        out_specs=[pl.BlockSpec((None, H, tm, LANES), lambda b, s: (b, 0, s, 0)),
                   pl.BlockSpec((None, H, tm, LANES), lambda b, s: (b, 0, s, 0)),
                   pl.BlockSpec((None, H, LANES + SUBLANES, tm), lambda b, s: (b, 0, 0, s))],
        out_shape=[jax.ShapeDtypeStruct((B, H, S, LANES), BF16),
                   jax.ShapeDtypeStruct((B, H, S, LANES), BF16),
                   jax.ShapeDtypeStruct((B, H, LANES + SUBLANES, S), BF16)],
        compiler_params=_params(2),
        name="qkv_proj",
    )(h3, kv_nw, a_nw, w_k, w_vt, w_q, cos, sin_lo, sin_hi)


def _diff_attn_kernel(lam_ref, q_ref, qn_ref, k_ref, kn_ref, vt_ref, swt_ref, o_ref,
                      sx_scr, s0_scr, s1_scr, cx_scr, c0_scr, c1_scr, m_scr, acc_scr, *, lam_init):
    bq = q_ref.shape[0]
    bk = bq
    hd = LANES // 2
    i = pl.program_id(2)
    s_bufs = (s0_scr, s1_scr, sx_scr)
    c_bufs = (c0_scr, c1_scr, cx_scr)
    X = 2

    def stacked(q):
        lane = lax.broadcasted_iota(jnp.int32, q.shape, 1)
        zero = jnp.zeros_like(q)
        return jnp.concatenate([jnp.where(lane < hd, q, zero), jnp.where(lane >= hd, q, zero)], axis=0)

    qs = stacked(q_ref[...])

    m_scr[...] = jnp.full_like(m_scr, -jnp.inf)
    acc_scr[...] = jnp.zeros_like(acc_scr)

    def score(k_rows, q_stacked, buf):
        s = _dot_nt(k_rows, q_stacked)
        s_bufs[buf][...] = s
        c_bufs[buf][...] = jnp.max(s, axis=0, keepdims=True)

    def qk(n, buf):
        score(k_ref[pl.ds(pl.multiple_of(n * bk, bk), bk), :], qs, buf)

    def softmax_pv(n, buf, masked):
        s = s_bufs[buf][...]
        m_prev = m_scr[...]
        if masked:
            kk = lax.broadcasted_iota(jnp.int32, (bk, bq), 0)
            qq = lax.broadcasted_iota(jnp.int32, (bk, bq), 1)
            keep = kk <= qq
            s = jnp.where(jnp.concatenate([keep, keep], axis=1), s, -jnp.inf)
            m_new = jnp.maximum(m_prev, jnp.max(s, axis=0, keepdims=True))
        else:
            m_new = jnp.maximum(m_prev, c_bufs[buf][...])
        alpha = jnp.exp2(m_prev - m_new)
        p = jnp.exp2(s - m_new).astype(BF16)
        cols = pl.ds(pl.multiple_of(n * bk, bk), bk)
        acc_scr[...] = alpha * acc_scr[...] + _dot(vt_ref[:, cols], p)
        m_scr[...] = m_new

    @pl.when((pl.program_id(0) == 0) & (pl.program_id(1) == 0) & (i == 0))
    def _():
        qk(0, X)

    @pl.when(i > 0)
    def _():
        qk(1, 1)
        softmax_pv(0, X, False)

    def stage(n, buf):
        qk(n + 1, 1 - buf)
        softmax_pv(n, buf, False)

    def body(n, carry):
        lax.cond(n % 2 == 1, lambda: stage(n, 1), lambda: stage(n, 0))
        return carry

    lax.fori_loop(1, i, body, 0)

    def tail(buf):
        softmax_pv(i, buf, True)
        score(kn_ref[...], stacked(qn_ref[...]), X)
        lp = lam_ref[...]
        lam = (jnp.exp(jnp.sum(lp[0:1, :] * lp[1:2, :], axis=-1, keepdims=True))
               - jnp.exp(jnp.sum(lp[2:3, :] * lp[3:4, :], axis=-1, keepdims=True)) + lam_init)
        acc = acc_scr[...]
        o_all = acc[:LANES, :] / acc[LANES:LANES + 1, :]
        ot = o_all[:, :bq] - lam * o_all[:, bq:]
        ms = jnp.mean(ot * ot, axis=0, keepdims=True)
        ot = ot * lax.rsqrt(ms + EPS) * swt_ref[...] * (1.0 - lam_init)
        o_ref[...] = ot.T.astype(BF16)

    lax.cond(i % 2 == 1, lambda: tail(1), lambda: lax.cond(i > 0, lambda: tail(0), lambda: tail(X)))


def _diff_attn(q4, k4, vt4, lam_p, subln_col, lam_init):
    B, H, S, _ = q4.shape
    bq = ATTN_BLOCK
    nq = S // bq

    def next_step(b, h, i):
        i2 = i + 1
        h2 = h + i2 // nq
        b2 = b + h2 // H
        last = b2 >= B
        return (jnp.where(last, b, b2), jnp.where(last, h, h2 % H), jnp.where(last, i, i2 % nq))

    def q_next(b, h, i):
        b2, h2, i2 = next_step(b, h, i)
        return (b2, h2, i2, 0)

    def k_next(b, h, i):
        b2, h2, _ = next_step(b, h, i)
        return (b2, h2, 0, 0)

    return pl.pallas_call(
        functools.partial(_diff_attn_kernel, lam_init=lam_init),
        grid=(B, H, nq),
        in_specs=[
            _full(lam_p.shape),
            pl.BlockSpec((None, None, bq, LANES), lambda b, h, i: (b, h, i, 0)),
            pl.BlockSpec((None, None, bq, LANES), q_next),
            pl.BlockSpec((None, None, S, LANES), lambda b, h, i: (b, h, 0, 0)),
            pl.BlockSpec((None, None, bq, LANES), k_next),
            pl.BlockSpec((None, None, LANES + SUBLANES, S), lambda b, h, i: (b, h, 0, 0)),
            _full(subln_col.shape),
        ],
        out_specs=pl.BlockSpec((None, None, bq, LANES), lambda b, h, i: (b, h, i, 0)),
        out_shape=jax.ShapeDtypeStruct((B, H, S, LANES), BF16),
        scratch_shapes=[
            pltpu.VMEM((bq, 2 * bq), F32),
            pltpu.VMEM((bq, 2 * bq), F32),
            pltpu.VMEM((bq, 2 * bq), F32),
            pltpu.VMEM((1, 2 * bq), F32),
            pltpu.VMEM((1, 2 * bq), F32),
            pltpu.VMEM((1, 2 * bq), F32),
            pltpu.VMEM((1, 2 * bq), F32),
            pltpu.VMEM((LANES + SUBLANES, 2 * bq), F32),
        ],
        compiler_params=_params(3),
        name="diff_attn",
    )(lam_p, q4, q4, k4, k4, vt4, subln_col)


def _rope_tables(seq, hd):
    pos = jnp.arange(seq, dtype=F32)
    inv = ROPE_THETA ** (-jnp.arange(0, hd, 2, dtype=F32) / hd)
    f = pos[:, None] * inv[None, :]
    cos_h, sin_h = jnp.cos(f), jnp.sin(f)
    zeros = jnp.zeros_like(sin_h)
    cos = jnp.concatenate([cos_h, cos_h] * 2, axis=-1)
    sin_lo = jnp.concatenate([-sin_h, zeros] * 2, axis=-1)
    sin_hi = jnp.concatenate([zeros, sin_h] * 2, axis=-1)
    return cos, sin_lo, sin_hi


def kernel(x, attn_norm_w, ffn_norm_w, gla_w_qkvg, gla_w_gk1, gla_w_gk2, gla_b_gk, gla_onorm_w, gla_w_o, kv_norm_w, w_kv, diff_w_q, diff_lambda, diff_subln_w, diff_w_o, ffn_w_in, ffn_conv_w, ffn_conv_b, ffn_w_out, final_norm_w):
    B, S, D = x.shape
    depth = attn_norm_w.shape[0]
    n_a = gla_w_qkvg.shape[0]
    dk = gla_w_gk2.shape[2]
    dv = gla_w_o.shape[1]
    n_qk = diff_w_q.shape[2]
    hd = n_qk // (2 * DIFF_HEADS)
    assert hd * 2 == LANES and (w_kv.shape[1] - n_qk) == DIFF_HEADS * LANES
    assert S % TOK_BLOCK == 0 and S % ATTN_BLOCK == 0 and TOK_BLOCK % GLA_CHUNK == 0
    assert ffn_w_out.shape[1] % FFN_CHUNK == 0 and GLA_GATE_RANK <= LANES

    row = lambda v: v.reshape(1, -1)
    cos, sin_lo, sin_hi = _rope_tables(S, hd)

    h = x
    for l in range(depth):
        final_nw = row(final_norm_w) if l == depth - 1 else None
        ffn_args = (row(ffn_norm_w[l]), ffn_w_in[l].astype(BF16), ffn_conv_w[l], row(ffn_conv_b[l]),
                    ffn_w_out[l].astype(BF16))
        if l < n_a:
            w1_pad = jnp.pad(gla_w_gk1[l], ((0, 0), (0, LANES - GLA_GATE_RANK)))
            w_cat = jnp.concatenate([gla_w_qkvg[l], w1_pad], axis=1).astype(BF16)
            w2_pad = jnp.pad(gla_w_gk2[l], ((0, LANES - GLA_GATE_RANK), (0, 0))).astype(BF16)
            qkvg, gk = _gla_proj(h.reshape(B * S, D), row(attn_norm_w[l]), w_cat, w2_pad,
                                 row(gla_b_gk[l]))
            h = _gla_core(qkvg.reshape(B, S, -1), gk.reshape(B, S, dk), h, row(gla_onorm_w[l]),
                          gla_w_o[l].astype(BF16), dk, dv)
            h = _ffn(h, *ffn_args, final_nw=final_nw)
        else:
            j = l - n_a
            lam_init = 0.8 - 0.6 * math.exp(-0.3 * l)
            q3, k_new, vt_new = _qkv_proj(h, row(kv_norm_w), row(attn_norm_w[l]),
                                          w_kv[:, :n_qk].astype(BF16), w_kv[:, n_qk:].T.astype(BF16),
                                          diff_w_q[j].astype(BF16), cos, sin_lo, sin_hi,
                                          float(hd) ** -0.5 * LOG2E)
            if l == n_a:
                k3, vt4 = k_new, vt_new
            o3 = _diff_attn(q3, k3, vt4, diff_lambda[j], diff_subln_w[j].reshape(-1, 1), lam_init)
            h = _ffn(h, *ffn_args, mix3=o3, wo=diff_w_o[j].astype(BF16), final_nw=final_nw)
    return h
```

```python
import functools
import math

import jax
import jax.numpy as jnp
from jax import lax
from jax.experimental import pallas as pl
from jax.experimental.pallas import tpu as pltpu

F32 = jnp.float32
BF16 = jnp.bfloat16

EPS = 1e-6
ROPE_THETA = 10000.0
GLA_HEADS = 4
GLA_CHUNK = 64
GLA_GATE_RANK = 16
GLA_GATE_NORM = 16.0
DIFF_HEADS = 8
CONV_WIDTH = 3

LANES = 128
SUBLANES = 8
VMEM_LIMIT = 56 * 1024 * 1024

TOK_BLOCK = 512
FFN_CHUNK = 256
ATTN_BLOCK = 512
LOG2E = math.log2(math.e)


def _params(n_axes):
    return pltpu.CompilerParams(
        dimension_semantics=("arbitrary",) * n_axes, vmem_limit_bytes=VMEM_LIMIT)


def _full(shape):
    nd = len(shape)
    return pl.BlockSpec(shape, lambda *_: (0,) * nd)


def _rms(x, w):
    ms = jnp.mean(x * x, axis=-1, keepdims=True)
    return x * lax.rsqrt(ms + EPS) * w


def _sigmoid(x):
    return 1.0 / (1.0 + jnp.exp2(x * (-LOG2E)))


def _dot(a, b):
    return jnp.dot(a, b, preferred_element_type=F32)


def _dot_nt(a, b):
    return lax.dot_general(a, b, (((1,), (1,)), ((), ())), preferred_element_type=F32)


def _dot_tn(a, b):
    return lax.dot_general(a, b, (((0,), (0,)), ((), ())), preferred_element_type=F32)


def _gla_proj_kernel(x_ref, nw_ref, w_ref, w2_ref, b_ref, qkvg_ref, gk_ref, *, n_main):
    hn = _rms(x_ref[...], nw_ref[...]).astype(BF16)
    proj = _dot(hn, w_ref[...])
    qkvg_ref[...] = proj[:, :n_main].astype(BF16)
    low = proj[:, n_main:].astype(BF16)
    z = _dot(low, w2_ref[...]) + b_ref[...]
    gk_ref[...] = (jnp.minimum(z, 0.0) - jnp.log(1.0 + jnp.exp(-jnp.abs(z)))) * (1.0 / GLA_GATE_NORM)


def _gla_proj(x2, nw, w_cat, w2_pad, b_gk):
    T, D = x2.shape
    n_main = w_cat.shape[1] - LANES
    dk = w2_pad.shape[1]
    tm = TOK_BLOCK
    return pl.pallas_call(
        functools.partial(_gla_proj_kernel, n_main=n_main),
        grid=(T // tm,),
        in_specs=[
            pl.BlockSpec((tm, D), lambda i: (i, 0)),
            _full(nw.shape), _full(w_cat.shape), _full(w2_pad.shape), _full(b_gk.shape),
        ],
        out_specs=[
            pl.BlockSpec((tm, n_main), lambda i: (i, 0)),
            pl.BlockSpec((tm, dk), lambda i: (i, 0)),
        ],
        out_shape=[
            jax.ShapeDtypeStruct((T, n_main), BF16),
            jax.ShapeDtypeStruct((T, dk), F32),
        ],
        compiler_params=_params(1),
        name="gla_proj",
    )(x2, nw, w_cat, w2_pad, b_gk)


def _cumsum_rows(g):
    n = g.shape[0]
    ridx = lax.broadcasted_iota(jnp.int32, g.shape, 0)
    b = g
    s = 1
    while s < n:
        b = b + jnp.where(ridx >= s, pltpu.roll(b, s, axis=0), 0.0)
        s *= 2
    return b


def _gla_core_kernel(qkvg_ref, gk_ref, x_ref, onw_ref, wo_ref, out_ref, state_ref, o_scr,
                     *, dk, dv):
    C = GLA_CHUNK
    H = GLA_HEADS
    hk = dk // H
    hv = dv // H
    tb = x_ref.shape[0]

    @pl.when(pl.program_id(1) == 0)
    def _():
        state_ref[...] = jnp.zeros_like(state_ref)

    row = lax.broadcasted_iota(jnp.int32, (C, C), 0)
    col = lax.broadcasted_iota(jnp.int32, (C, C), 1)
    causal = col <= row
    qscale = hk ** -0.5

    for c in range(tb // C):
        r = pl.ds(c * C, C)
        b_all = _cumsum_rows(gk_ref[r, :])
        for h in range(H):
            q = qkvg_ref[r, h * hk:(h + 1) * hk].astype(F32)
            k = qkvg_ref[r, dk + h * hk:dk + (h + 1) * hk].astype(F32)
            v = qkvg_ref[r, 2 * dk + h * hv:2 * dk + (h + 1) * hv]
            b = b_all[:, h * hk:(h + 1) * hk]
            b_last = b[C - 1:C, :]
            q_in = (q * qscale * jnp.exp(b)).astype(BF16)
            k_in = (k * jnp.exp(-b)).astype(BF16)
            k_end = (k * jnp.exp(b_last - b)).astype(BF16)
            scores = jnp.where(causal, _dot_nt(q_in, k_in), 0.0)
            st = state_ref[h]
            o = _dot(scores.astype(BF16), v) + _dot_nt(q_in, st.astype(BF16))
            o_scr[r, h * hv:(h + 1) * hv] = o
            state_ref[h] = st * jnp.exp(b_last) + _dot_tn(v, k_end)

    onw = onw_ref[...]
    parts = []
    for h in range(H):
        parts.append(_rms(o_scr[:, h * hv:(h + 1) * hv], onw))
    on = jnp.concatenate(parts, axis=1)
    g = qkvg_ref[:, 2 * dk + dv:2 * dk + 2 * dv].astype(F32)
    gated = (on * (g * _sigmoid(g))).astype(BF16)
    out_ref[...] = x_ref[...] + _dot(gated, wo_ref[...])


def _gla_core(qkvg3, gk3, x3, onw, wo, dk, dv):
    B, S, D = x3.shape
    tb = TOK_BLOCK
    H = GLA_HEADS
    return pl.pallas_call(
        functools.partial(_gla_core_kernel, dk=dk, dv=dv),
        grid=(B, S // tb),
        in_specs=[
            pl.BlockSpec((None, tb, qkvg3.shape[2]), lambda b, s: (b, s, 0)),
            pl.BlockSpec((None, tb, dk), lambda b, s: (b, s, 0)),
            pl.BlockSpec((None, tb, D), lambda b, s: (b, s, 0)),
            _full(onw.shape), _full(wo.shape),
        ],
        out_specs=pl.BlockSpec((None, tb, D), lambda b, s: (b, s, 0)),
        out_shape=jax.ShapeDtypeStruct((B, S, D), F32),
        scratch_shapes=[
            pltpu.VMEM((H, dv // H, dk // H), F32),
            pltpu.VMEM((tb, dv), F32),
        ],
        compiler_params=_params(2),
        name="gla_core",
    )(qkvg3, gk3, x3, onw, wo)


def _ffn_kernel(*refs, has_mix, final_norm, d_ff):
    refs = list(refs)
    h_ref = refs.pop(0)
    if has_mix:
        mix_ref = refs.pop(0)
        wo_ref = refs.pop(0)
    nw_ref, win_ref, cw_ref, cb_ref, wout_ref = refs[:5]
    refs = refs[5:]
    if final_norm:
        fnw_ref = refs.pop(0)
    out_ref, perm_ref, u_ref, carry_ref, act_ref = refs

    tm, d_model = h_ref.shape
    fc = FFN_CHUNK
    sub = SUBLANES
    seg = tm // sub
    pitch = seg + sub
    n_slabs = d_model // LANES

    @pl.when(pl.program_id(1) == 0)
    def _():
        carry_ref[...] = jnp.zeros_like(carry_ref)

    h = h_ref[...]
    if has_mix:
        mix = jnp.concatenate([mix_ref[i] for i in range(mix_ref.shape[0])], axis=1)
        h = h + _dot(mix, wo_ref[...])

    def interleave(x):
        for c in range(n_slabs):
            for s_ in range(sub):
                perm_ref[c, pl.ds(s_ * pitch, seg), :] = x[s_ * seg:(s_ + 1) * seg, c * LANES:(c + 1) * LANES]
        cols = []
        for c in range(n_slabs):
            cols.append(jnp.concatenate(
                [perm_ref[c, pl.ds(j, sub, stride=pitch), :] for j in range(seg)], axis=0))
        return jnp.concatenate(cols, axis=1)

    def deinterleave_store(y):
        for c in range(n_slabs):
            for j in range(seg):
                perm_ref[c, pl.ds(j, sub, stride=pitch), :] = y[j * sub:(j + 1) * sub, c * LANES:(c + 1) * LANES]
        for c in range(n_slabs):
            for s_ in range(sub):
                out_ref[s_ * seg:(s_ + 1) * seg, c * LANES:(c + 1) * LANES] = perm_ref[c, pl.ds(s_ * pitch, seg), :]

    h = interleave(h)
    hn = _rms(h, nw_ref[...]).astype(BF16)
    first_row = lax.broadcasted_iota(jnp.int32, (sub, fc), 0) == 0

    def up(j):
        for part in range(2):
            u_ref[2 * j + part] = _dot(hn, win_ref[:, pl.ds(part * d_ff + j * fc, fc)])

    def shift(y, e, k):
        top = pltpu.roll(y[tm - sub:, :], 1, axis=0)
        fixed = jnp.where(first_row, carry_ref[e, k], top)
        carry_ref[e, k] = top
        return jnp.concatenate([fixed, y[:tm - sub, :]], axis=0)

    def conv(j, part):
        e = 2 * j + part
        cols = pl.ds(part * d_ff + j * fc, fc)
        u = u_ref[e]
        cw = cw_ref[:, cols]
        y = shift(u * cw[0:1, :], e, 0) + u * cw[1:2, :]
        return shift(y, e, 1) + u * cw[2:3, :] + cb_ref[:, cols]

    n_chunks = d_ff // fc
    up(0)
    for j in range(n_chunks):
        if j + 1 < n_chunks:
            up(j + 1)
        a = conv(j, 0)
        g = conv(j, 1)
        act_ref[:, pl.ds(j * fc, fc)] = (g * _sigmoid(g) * a).astype(BF16)
    out = h + _dot(act_ref[...], wout_ref[...])
    if final_norm:
        out = _rms(out, fnw_ref[...])
    deinterleave_store(out)


def _ffn(h3, nw, w_in, conv_w, conv_b, w_out, mix3=None, wo=None, final_nw=None):
    B, S, D = h3.shape
    tm = TOK_BLOCK
    d_ff = w_out.shape[0]
    has_mix = mix3 is not None
    final_norm = final_nw is not None
    tok = pl.BlockSpec((None, tm, D), lambda b, s: (b, s, 0))
    args, specs = [h3], [tok]
    if has_mix:
        args += [mix3, wo]
        specs += [pl.BlockSpec((None, mix3.shape[1], tm, mix3.shape[3]), lambda b, s: (b, 0, s, 0)),
                  _full(wo.shape)]
    args += [nw, w_in, conv_w, conv_b, w_out]
    specs += [_full(nw.shape), _full(w_in.shape), _full(conv_w.shape), _full(conv_b.shape),
              _full(w_out.shape)]
    if final_norm:
        args.append(final_nw)
        specs.append(_full(final_nw.shape))
    n_chunks = d_ff // FFN_CHUNK
    return pl.pallas_call(
        functools.partial(_ffn_kernel, has_mix=has_mix, final_norm=final_norm, d_ff=d_ff),
        grid=(B, S // tm),
        in_specs=specs,
        out_specs=tok,
        out_shape=jax.ShapeDtypeStruct((B, S, D), F32),
        scratch_shapes=[pltpu.VMEM((D // LANES, tm + SUBLANES * SUBLANES, LANES), F32),
                        pltpu.VMEM((2 * n_chunks, tm, FFN_CHUNK), F32),
                        pltpu.VMEM((2 * n_chunks, 2, SUBLANES, FFN_CHUNK), F32),
                        pltpu.VMEM((tm, d_ff), BF16)],
        compiler_params=_params(2),
        name="ffn_mix" if has_mix else "ffn",
    )(*args)


def _rope(x, cos, sin_lo, sin_hi):
    half = LANES // 4
    return (x * cos + pltpu.roll(x, LANES - half, axis=1) * sin_lo
            + pltpu.roll(x, half, axis=1) * sin_hi)


def _qkv_proj_kernel(h_ref, kvnw_ref, anw_ref, wk_ref, wvt_ref, wq_ref, cos_ref, slo_ref, shi_ref,
                     q_ref, k_ref, vt_ref, *, q_scale):
    x = h_ref[...]
    ms = jnp.mean(x * x, axis=-1, keepdims=True)
    xn = x * lax.rsqrt(ms + EPS)
    kvn = (xn * kvnw_ref[...]).astype(BF16)
    an = (xn * anw_ref[...]).astype(BF16)
    k = _dot(kvn, wk_ref[...])
    q = _dot(an, wq_ref[...])
    vt = _dot_nt(wvt_ref[...], kvn)
    cos = cos_ref[...]
    slo = slo_ref[...]
    shi = shi_ref[...]
    for j in range(q.shape[1] // LANES):
        cols = slice(j * LANES, (j + 1) * LANES)
        q_ref[j] = (_rope(q[:, cols], cos, slo, shi) * q_scale).astype(BF16)
        k_ref[j] = _rope(k[:, cols], cos, slo, shi).astype(BF16)
    ones = jnp.ones((SUBLANES, vt.shape[1]), BF16)
    for h in range(vt_ref.shape[0]):
        vt_ref[h, :LANES, :] = vt[h * LANES:(h + 1) * LANES, :].astype(BF16)
        vt_ref[h, LANES:, :] = ones


def _qkv_proj(h3, kv_nw, a_nw, w_k, w_vt, w_q, cos, sin_lo, sin_hi, q_scale):
    B, S, D = h3.shape
    tm = TOK_BLOCK
    n_qk = w_q.shape[1]
    H = w_vt.shape[0] // LANES
    tok = lambda n: pl.BlockSpec((None, tm, n), lambda b, s: (b, s, 0))
    tab = pl.BlockSpec((tm, LANES), lambda b, s: (s, 0))
    return pl.pallas_call(
        functools.partial(_qkv_proj_kernel, q_scale=q_scale),
        grid=(B, S // tm),
        in_specs=[tok(D), _full(kv_nw.shape), _full(a_nw.shape), _full(w_k.shape), _full(w_vt.shape),
                  _full(w_q.shape), tab, tab, tab],
        out_specs=[pl.BlockSpec((None, H, tm, LANES), lambda b, s: (b, 0, s, 0)),
                   pl.BlockSpec((None, H, tm, LANES), lambda b, s: (b, 0, s, 0)),
                   pl.BlockSpec((None, H, LANES + SUBLANES, tm), lambda b, s: (b, 0, 0, s))],
        out_shape=[jax.ShapeDtypeStruct((B, H, S, LANES), BF16),
                   jax.ShapeDtypeStruct((B, H, S, LANES), BF16),
                   jax.ShapeDtypeStruct((B, H, LANES + SUBLANES, S), BF16)],
        compiler_params=_params(2),
        name="qkv_proj",
    )(h3, kv_nw, a_nw, w_k, w_vt, w_q, cos, sin_lo, sin_hi)


def _diff_attn_kernel(lam_ref, q_ref, qn_ref, k_ref, kn_ref, vt_ref, swt_ref, o_ref,
                      sx_scr, s0_scr, s1_scr, cx_scr, c0_scr, c1_scr, m_scr, acc_scr, *, lam_init):
    bq = q_ref.shape[0]
    bk = bq
    hd = LANES // 2
    i = pl.program_id(2)
    s_bufs = (s0_scr, s1_scr, sx_scr)
    c_bufs = (c0_scr, c1_scr, cx_scr)
    X = 2

    def stacked(q):
        lane = lax.broadcasted_iota(jnp.int32, q.shape, 1)
        zero = jnp.zeros_like(q)
        return jnp.concatenate([jnp.where(lane < hd, q, zero), jnp.where(lane >= hd, q, zero)], axis=0)

    qs = stacked(q_ref[...])

    m_scr[...] = jnp.full_like(m_scr, -jnp.inf)
    acc_scr[...] = jnp.zeros_like(acc_scr)

    def score(k_rows, q_stacked, buf):
        s = _dot_nt(k_rows, q_stacked)
        s_bufs[buf][...] = s
        c_bufs[buf][...] = jnp.max(s, axis=0, keepdims=True)

    def qk(n, buf):
        score(k_ref[pl.ds(pl.multiple_of(n * bk, bk), bk), :], qs, buf)

    def softmax_pv(n, buf, masked):
        s = s_bufs[buf][...]
        m_prev = m_scr[...]
        if masked:
            kk = lax.broadcasted_iota(jnp.int32, (bk, bq), 0)
            qq = lax.broadcasted_iota(jnp.int32, (bk, bq), 1)
            keep = kk <= qq
            s = jnp.where(jnp.concatenate([keep, keep], axis=1), s, -jnp.inf)
            m_new = jnp.maximum(m_prev, jnp.max(s, axis=0, keepdims=True))
        else:
            m_new = jnp.maximum(m_prev, c_bufs[buf][...])
        alpha = jnp.exp2(m_prev - m_new)
        p = jnp.exp2(s - m_new).astype(BF16)
        cols = pl.ds(pl.multiple_of(n * bk, bk), bk)
        acc_scr[...] = alpha * acc_scr[...] + _dot(vt_ref[:, cols], p)
        m_scr[...] = m_new

    @pl.when((pl.program_id(0) == 0) & (pl.program_id(1) == 0) & (i == 0))
    def _():
        qk(0, X)

    @pl.when(i > 0)
    def _():
        qk(1, 1)
        softmax_pv(0, X, False)

    def stage(n, buf):
        qk(n + 1, 1 - buf)
        softmax_pv(n, buf, False)

    def body(n, carry):
        lax.cond(n % 2 == 1, lambda: stage(n, 1), lambda: stage(n, 0))
        return carry

    lax.fori_loop(1, i, body, 0)

    def tail(buf):
        softmax_pv(i, buf, True)
        score(kn_ref[...], stacked(qn_ref[...]), X)
        lp = lam_ref[...]
        lam = (jnp.exp(jnp.sum(lp[0:1, :] * lp[1:2, :], axis=-1, keepdims=True))
               - jnp.exp(jnp.sum(lp[2:3, :] * lp[3:4, :], axis=-1, keepdims=True)) + lam_init)
        acc = acc_scr[...]
        o_all = acc[:LANES, :] / acc[LANES:LANES + 1, :]
        ot = o_all[:, :bq] - lam * o_all[:, bq:]
        ms = jnp.mean(ot * ot, axis=0, keepdims=True)
        ot = ot * lax.rsqrt(ms + EPS) * swt_ref[...] * (1.0 - lam_init)
        o_ref[...] = ot.T.astype(BF16)

    lax.cond(i % 2 == 1, lambda: tail(1), lambda: lax.cond(i > 0, lambda: tail(0), lambda: tail(X)))


def _diff_attn(q4, k4, vt4, lam_p, subln_col, lam_init):
    B, H, S, _ = q4.shape
    bq = ATTN_BLOCK
    nq = S // bq

    def next_step(b, h, i):
        i2 = i + 1
        h2 = h + i2 // nq
        b2 = b + h2 // H
        last = b2 >= B
        return (jnp.where(last, b, b2), jnp.where(last, h, h2 % H), jnp.where(last, i, i2 % nq))

    def q_next(b, h, i):
        b2, h2, i2 = next_step(b, h, i)
        return (b2, h2, i2, 0)

    def k_next(b, h, i):
        b2, h2, _ = next_step(b, h, i)
        return (b2, h2, 0, 0)

    return pl.pallas_call(
        functools.partial(_diff_attn_kernel, lam_init=lam_init),
        grid=(B, H, nq),
        in_specs=[
            _full(lam_p.shape),
            pl.BlockSpec((None, None, bq, LANES), lambda b, h, i: (b, h, i, 0)),
            pl.BlockSpec((None, None, bq, LANES), q_next),
            pl.BlockSpec((None, None, S, LANES), lambda b, h, i: (b, h, 0, 0)),
            pl.BlockSpec((None, None, bq, LANES), k_next),
            pl.BlockSpec((None, None, LANES + SUBLANES, S), lambda b, h, i: (b, h, 0, 0)),
            _full(subln_col.shape),
        ],
        out_specs=pl.BlockSpec((None, None, bq, LANES), lambda b, h, i: (b, h, i, 0)),
        out_shape=jax.ShapeDtypeStruct((B, H, S, LANES), BF16),
        scratch_shapes=[
            pltpu.VMEM((bq, 2 * bq), F32),
            pltpu.VMEM((bq, 2 * bq), F32),
            pltpu.VMEM((bq, 2 * bq), F32),
            pltpu.VMEM((1, 2 * bq), F32),
            pltpu.VMEM((1, 2 * bq), F32),
            pltpu.VMEM((1, 2 * bq), F32),
            pltpu.VMEM((1, 2 * bq), F32),
            pltpu.VMEM((LANES + SUBLANES, 2 * bq), F32),
        ],
        compiler_params=_params(3),
        name="diff_attn",
    )(lam_p, q4, q4, k4, k4, vt4, subln_col)


def _rope_tables(seq, hd):
    pos = jnp.arange(seq, dtype=F32)
    inv = ROPE_THETA ** (-jnp.arange(0, hd, 2, dtype=F32) / hd)
    f = pos[:, None] * inv[None, :]
    cos_h, sin_h = jnp.cos(f), jnp.sin(f)
    zeros = jnp.zeros_like(sin_h)
    cos = jnp.concatenate([cos_h, cos_h] * 2, axis=-1)
    sin_lo = jnp.concatenate([-sin_h, zeros] * 2, axis=-1)
    sin_hi = jnp.concatenate([zeros, sin_h] * 2, axis=-1)
    return cos, sin_lo, sin_hi


def kernel(x, attn_norm_w, ffn_norm_w, gla_w_qkvg, gla_w_gk1, gla_w_gk2, gla_b_gk, gla_onorm_w, gla_w_o, kv_norm_w, w_kv, diff_w_q, diff_lambda, diff_subln_w, diff_w_o, ffn_w_in, ffn_conv_w, ffn_conv_b, ffn_w_out, final_norm_w):
    B, S, D = x.shape
    depth = attn_norm_w.shape[0]
    n_a = gla_w_qkvg.shape[0]
    dk = gla_w_gk2.shape[2]
    dv = gla_w_o.shape[1]
    n_qk = diff_w_q.shape[2]
    hd = n_qk // (2 * DIFF_HEADS)
    assert hd * 2 == LANES and (w_kv.shape[1] - n_qk) == DIFF_HEADS * LANES
    assert S % TOK_BLOCK == 0 and S % ATTN_BLOCK == 0 and TOK_BLOCK % GLA_CHUNK == 0
    assert ffn_w_out.shape[1] % FFN_CHUNK == 0 and GLA_GATE_RANK <= LANES

    row = lambda v: v.reshape(1, -1)
    cos, sin_lo, sin_hi = _rope_tables(S, hd)

    h = x
    for l in range(depth):
        final_nw = row(final_norm_w) if l == depth - 1 else None
        ffn_args = (row(ffn_norm_w[l]), ffn_w_in[l].astype(BF16), ffn_conv_w[l], row(ffn_conv_b[l]),
                    ffn_w_out[l].astype(BF16))
        if l < n_a:
            w1_pad = jnp.pad(gla_w_gk1[l], ((0, 0), (0, LANES - GLA_GATE_RANK)))
            w_cat = jnp.concatenate([gla_w_qkvg[l], w1_pad], axis=1).astype(BF16)
            w2_pad = jnp.pad(gla_w_gk2[l], ((0, LANES - GLA_GATE_RANK), (0, 0))).astype(BF16)
            qkvg, gk = _gla_proj(h.reshape(B * S, D), row(attn_norm_w[l]), w_cat, w2_pad,
                                 row(gla_b_gk[l]))
            h = _gla_core(qkvg.reshape(B, S, -1), gk.reshape(B, S, dk), h, row(gla_onorm_w[l]),
                          gla_w_o[l].astype(BF16), dk, dv)
            h = _ffn(h, *ffn_args, final_nw=final_nw)
        else:
            j = l - n_a
            lam_init = 0.8 - 0.6 * math.exp(-0.3 * l)
            q3, k_new, vt_new = _qkv_proj(h, row(kv_norm_w), row(attn_norm_w[l]),
                                          w_kv[:, :n_qk].astype(BF16), w_kv[:, n_qk:].T.astype(BF16),
                                          diff_w_q[j].astype(BF16), cos, sin_lo, sin_hi,
                                          float(hd) ** -0.5 * LOG2E)
            if l == n_a:
                k3, vt4 = k_new, vt_new
            o3 = _diff_attn(q3, k3, vt4, diff_lambda[j], diff_subln_w[j].reshape(-1, 1), lam_init)
            h = _ffn(h, *ffn_args, mix3=o3, wo=diff_w_o[j].astype(BF16), final_nw=final_nw)
    return h
```

```python
import functools
import math

import jax
import jax.numpy as jnp
from jax import lax
from jax.experimental import pallas as pl
from jax.experimental.pallas import tpu as pltpu

F32 = jnp.float32
BF16 = jnp.bfloat16

EPS = 1e-6
ROPE_THETA = 10000.0
GLA_HEADS = 4
GLA_CHUNK = 64
GLA_GATE_RANK = 16
GLA_GATE_NORM = 16.0
DIFF_HEADS = 8
CONV_WIDTH = 3

LANES = 128
SUBLANES = 8
VMEM_LIMIT = 56 * 1024 * 1024

TOK_BLOCK = 512
FFN_CHUNK = 256
ATTN_BLOCK = 512
LOG2E = math.log2(math.e)


def _params(n_axes):
    return pltpu.CompilerParams(
        dimension_semantics=("arbitrary",) * n_axes, vmem_limit_bytes=VMEM_LIMIT)


def _full(shape):
    nd = len(shape)
    return pl.BlockSpec(shape, lambda *_: (0,) * nd)


def _layer_slab(shape, layer):
    return pl.BlockSpec((None,) + tuple(shape[1:]), lambda *_: (layer, 0, 0), pipeline_mode=pl.Buffered(1))


def _rms(x, w):
    ms = jnp.mean(x * x, axis=-1, keepdims=True)
    return x * lax.rsqrt(ms + EPS) * w


def _sigmoid(x):
    return 1.0 / (1.0 + jnp.exp2(x * (-LOG2E)))


def _dot(a, b):
    return jnp.dot(a, b, preferred_element_type=F32)


def _dot_nt(a, b):
    return lax.dot_general(a, b, (((1,), (1,)), ((), ())), preferred_element_type=F32)


def _dot_tn(a, b):
    return lax.dot_general(a, b, (((0,), (0,)), ((), ())), preferred_element_type=F32)


def _gla_proj_kernel(x_ref, nw_ref, w_ref, w2_ref, b_ref, qkvg_ref, gk_ref, *, n_main):
    hn = _rms(x_ref[...], nw_ref[...]).astype(BF16)
    proj = _dot(hn, w_ref[...])
    qkvg_ref[...] = proj[:, :n_main].astype(BF16)
    low = proj[:, n_main:].astype(BF16)
    z = _dot(low, w2_ref[...]) + b_ref[...]
    gk_ref[...] = (jnp.minimum(z, 0.0) - jnp.log(1.0 + jnp.exp(-jnp.abs(z)))) * (1.0 / GLA_GATE_NORM)


def _gla_proj(x2, nw, w_cat, w2_pad, b_gk):
    T, D = x2.shape
    n_main = w_cat.shape[1] - LANES
    dk = w2_pad.shape[1]
    tm = TOK_BLOCK
    return pl.pallas_call(
        functools.partial(_gla_proj_kernel, n_main=n_main),
        grid=(T // tm,),
        in_specs=[
            pl.BlockSpec((tm, D), lambda i: (i, 0)),
            _full(nw.shape), _full(w_cat.shape), _full(w2_pad.shape), _full(b_gk.shape),
        ],
        out_specs=[
            pl.BlockSpec((tm, n_main), lambda i: (i, 0)),
            pl.BlockSpec((tm, dk), lambda i: (i, 0)),
        ],
        out_shape=[
            jax.ShapeDtypeStruct((T, n_main), BF16),
            jax.ShapeDtypeStruct((T, dk), F32),
        ],
        compiler_params=_params(1),
        name="gla_proj",
    )(x2, nw, w_cat, w2_pad, b_gk)


def _cumsum_rows(g):
    n = g.shape[0]
    ridx = lax.broadcasted_iota(jnp.int32, g.shape, 0)
    b = g
    s = 1
    while s < n:
        b = b + jnp.where(ridx >= s, pltpu.roll(b, s, axis=0), 0.0)
        s *= 2
    return b


def _gla_core_kernel(qkvg_ref, gk_ref, x_ref, onw_ref, wo_ref, out_ref, state_ref, o_scr,
                     *, dk, dv):
    C = GLA_CHUNK
    H = GLA_HEADS
    hk = dk // H
    hv = dv // H
    tb = x_ref.shape[0]

    @pl.when(pl.program_id(1) == 0)
    def _():
        state_ref[...] = jnp.zeros_like(state_ref)

    row = lax.broadcasted_iota(jnp.int32, (C, C), 0)
    col = lax.broadcasted_iota(jnp.int32, (C, C), 1)
    causal = col <= row
    qscale = hk ** -0.5

    for c in range(tb // C):
        r = pl.ds(c * C, C)
        b_all = _cumsum_rows(gk_ref[r, :])
        for h in range(H):
            q = qkvg_ref[r, h * hk:(h + 1) * hk].astype(F32)
            k = qkvg_ref[r, dk + h * hk:dk + (h + 1) * hk].astype(F32)
            v = qkvg_ref[r, 2 * dk + h * hv:2 * dk + (h + 1) * hv]
            b = b_all[:, h * hk:(h + 1) * hk]
            b_last = b[C - 1:C, :]
            q_in = (q * qscale * jnp.exp(b)).astype(BF16)
            k_in = (k * jnp.exp(-b)).astype(BF16)
            k_end = (k * jnp.exp(b_last - b)).astype(BF16)
            scores = jnp.where(causal, _dot_nt(q_in, k_in), 0.0)
            st = state_ref[h]
            o = _dot(scores.astype(BF16), v) + _dot_nt(q_in, st.astype(BF16))
            o_scr[r, h * hv:(h + 1) * hv] = o
            state_ref[h] = st * jnp.exp(b_last) + _dot_tn(v, k_end)

    onw = onw_ref[...]
    parts = []
    for h in range(H):
        parts.append(_rms(o_scr[:, h * hv:(h + 1) * hv], onw))
    on = jnp.concatenate(parts, axis=1)
    g = qkvg_ref[:, 2 * dk + dv:2 * dk + 2 * dv].astype(F32)
    gated = (on * (g * _sigmoid(g))).astype(BF16)
    out_ref[...] = x_ref[...] + _dot(gated, wo_ref[...])


def _gla_core(qkvg3, gk3, x3, onw, wo, dk, dv):
    B, S, D = x3.shape
    tb = TOK_BLOCK
    H = GLA_HEADS
    return pl.pallas_call(
        functools.partial(_gla_core_kernel, dk=dk, dv=dv),
        grid=(B, S // tb),
        in_specs=[
            pl.BlockSpec((None, tb, qkvg3.shape[2]), lambda b, s: (b, s, 0)),
            pl.BlockSpec((None, tb, dk), lambda b, s: (b, s, 0)),
            pl.BlockSpec((None, tb, D), lambda b, s: (b, s, 0)),
            _full(onw.shape), _full(wo.shape),
        ],
        out_specs=pl.BlockSpec((None, tb, D), lambda b, s: (b, s, 0)),
        out_shape=jax.ShapeDtypeStruct((B, S, D), F32),
        scratch_shapes=[
            pltpu.VMEM((H, dv // H, dk // H), F32),
            pltpu.VMEM((tb, dv), F32),
        ],
        compiler_params=_params(2),
        name="gla_core",
    )(qkvg3, gk3, x3, onw, wo)


def _ffn_kernel(*refs, has_mix, final_norm, d_ff):
    refs = list(refs)
    h_ref = refs.pop(0)
    if has_mix:
        mix_ref = refs.pop(0)
        wo_ref = refs.pop(0)
    nw_ref, win_ref, cw_ref, cb_ref, wout_ref = refs[:5]
    refs = refs[5:]
    if final_norm:
        fnw_ref = refs.pop(0)
    out_ref, perm_ref, u_ref, carry_ref, act_ref = refs

    tm, d_model = h_ref.shape
    fc = FFN_CHUNK
    sub = SUBLANES
    seg = tm // sub
    pitch = seg + sub
    n_slabs = d_model // LANES

    @pl.when(pl.program_id(1) == 0)
    def _():
        carry_ref[...] = jnp.zeros_like(carry_ref)

    h = h_ref[...]
    if has_mix:
        mix = jnp.concatenate([mix_ref[i] for i in range(mix_ref.shape[0])], axis=1)
        h = h + _dot(mix, wo_ref[...])

    def interleave(x):
        for c in range(n_slabs):
            for s_ in range(sub):
                perm_ref[c, pl.ds(s_ * pitch, seg), :] = x[s_ * seg:(s_ + 1) * seg, c * LANES:(c + 1) * LANES]
        cols = []
        for c in range(n_slabs):
            cols.append(jnp.concatenate(
                [perm_ref[c, pl.ds(j, sub, stride=pitch), :] for j in range(seg)], axis=0))
        return jnp.concatenate(cols, axis=1)

    def deinterleave_store(y):
        for c in range(n_slabs):
            for j in range(seg):
                perm_ref[c, pl.ds(j, sub, stride=pitch), :] = y[j * sub:(j + 1) * sub, c * LANES:(c + 1) * LANES]
        for c in range(n_slabs):
            for s_ in range(sub):
                out_ref[s_ * seg:(s_ + 1) * seg, c * LANES:(c + 1) * LANES] = perm_ref[c, pl.ds(s_ * pitch, seg), :]

    h = interleave(h)
    hn = _rms(h, nw_ref[...]).astype(BF16)
    first_row = lax.broadcasted_iota(jnp.int32, (sub, fc), 0) == 0

    def up(j):
        for part in range(2):
            u_ref[2 * j + part] = _dot(hn, win_ref[:, pl.ds(part * d_ff + j * fc, fc)])

    def shift(y, e, k):
        top = pltpu.roll(y[tm - sub:, :], 1, axis=0)
        fixed = jnp.where(first_row, carry_ref[e, k], top)
        carry_ref[e, k] = top
        return jnp.concatenate([fixed, y[:tm - sub, :]], axis=0)

    def conv(j, part):
        e = 2 * j + part
        cols = pl.ds(part * d_ff + j * fc, fc)
        u = u_ref[e]
        cw = cw_ref[:, cols]
        y = shift(u * cw[0:1, :], e, 0) + u * cw[1:2, :]
        return shift(y, e, 1) + u * cw[2:3, :] + cb_ref[:, cols]

    n_chunks = d_ff // fc
    up(0)
    for j in range(n_chunks):
        if j + 1 < n_chunks:
            up(j + 1)
        a = conv(j, 0)
        g = conv(j, 1)
        act_ref[:, pl.ds(j * fc, fc)] = (g * _sigmoid(g) * a).astype(BF16)
    out = h + _dot(act_ref[...], wout_ref[...])
    if final_norm:
        out = _rms(out, fnw_ref[...])
    deinterleave_store(out)


def _ffn(h3, layer, nw, w_in, conv_w, conv_b, w_out, mix3=None, wo=None, final_nw=None):
    B, S, D = h3.shape
    tm = TOK_BLOCK
    d_ff = w_out.shape[1]
    has_mix = mix3 is not None
    final_norm = final_nw is not None
    tok = pl.BlockSpec((None, tm, D), lambda b, s: (b, s, 0))
    args, specs = [h3], [tok]
    if has_mix:
        args += [mix3, wo]
        specs += [pl.BlockSpec((None, mix3.shape[1], tm, mix3.shape[3]), lambda b, s: (b, 0, s, 0)),
                  _full(wo.shape)]
    args += [nw, w_in, conv_w, conv_b, w_out]
    specs += [_layer_slab(a.shape, layer) for a in (nw, w_in, conv_w, conv_b, w_out)]
    if final_norm:
        args.append(final_nw)
        specs.append(_full(final_nw.shape))
    n_chunks = d_ff // FFN_CHUNK
    return pl.pallas_call(
        functools.partial(_ffn_kernel, has_mix=has_mix, final_norm=final_norm, d_ff=d_ff),
        grid=(B, S // tm),
        in_specs=specs,
        out_specs=tok,
        out_shape=jax.ShapeDtypeStruct((B, S, D), F32),
        scratch_shapes=[pltpu.VMEM((D // LANES, tm + SUBLANES * SUBLANES, LANES), F32),
                        pltpu.VMEM((2 * n_chunks, tm, FFN_CHUNK), F32),
                        pltpu.VMEM((2 * n_chunks, 2, SUBLANES, FFN_CHUNK), F32),
                        pltpu.VMEM((tm, d_ff), BF16)],
        compiler_params=_params(2),
        name="ffn_mix" if has_mix else "ffn",
    )(*args)


def _rope(x, cos, sin_lo, sin_hi):
    half = LANES // 4
    return (x * cos + pltpu.roll(x, LANES - half, axis=1) * sin_lo
            + pltpu.roll(x, half, axis=1) * sin_hi)


def _qkv_proj_kernel(h_ref, kvnw_ref, anw_ref, wk_ref, wvt_ref, wq_ref, cos_ref, slo_ref, shi_ref,
                     q_ref, k_ref, vt_ref, *, q_scale):
    x = h_ref[...]
    ms = jnp.mean(x * x, axis=-1, keepdims=True)
    xn = x * lax.rsqrt(ms + EPS)
    kvn = (xn * kvnw_ref[...]).astype(BF16)
    an = (xn * anw_ref[...]).astype(BF16)
    k = _dot(kvn, wk_ref[...])
    q = _dot(an, wq_ref[...])
    vt = _dot_nt(wvt_ref[...], kvn)
    cos = cos_ref[...]
    slo = slo_ref[...]
    shi = shi_ref[...]
    for j in range(q.shape[1] // LANES):
        cols = slice(j * LANES, (j + 1) * LANES)
        q_ref[j] = (_rope(q[:, cols], cos, slo, shi) * q_scale).astype(BF16)
        k_ref[j] = _rope(k[:, cols], cos, slo, shi).astype(BF16)
    ones = jnp.ones((SUBLANES, vt.shape[1]), BF16)
    for h in range(vt_ref.shape[0]):
        vt_ref[h, :LANES, :] = vt[h * LANES:(h + 1) * LANES, :].astype(BF16)
        vt_ref[h, LANES:, :] = ones


def _qkv_proj(h3, kv_nw, a_nw, w_k, w_vt, w_q, cos, sin_lo, sin_hi, q_scale):
    B, S, D = h3.shape
    tm = TOK_BLOCK
    n_qk = w_q.shape[1]
    H = w_vt.shape[0] // LANES
    tok = lambda n: pl.BlockSpec((None, tm, n), lambda b, s: (b, s, 0))
    tab = pl.BlockSpec((tm, LANES), lambda b, s: (s, 0))
    return pl.pallas_call(
        functools.partial(_qkv_proj_kernel, q_scale=q_scale),
        grid=(B, S // tm),
        in_specs=[tok(D), _full(kv_nw.shape), _full(a_nw.shape), _full(w_k.shape), _full(w_vt.shape),
                  _full(w_q.shape), tab, tab, tab],
        out_specs=[pl.BlockSpec((None, H, tm, LANES), lambda b, s: (b, 0, s, 0)),
                   pl.BlockSpec((None, H, tm, LANES), lambda b, s: (b, 0, s, 0)),
                   pl.BlockSpec((None, H, LANES + SUBLANES, tm), lambda b, s: (b, 0, 0, s))],
        out_shape=[jax.ShapeDtypeStruct((B, H, S, LANES), BF16),
                   jax.ShapeDtypeStruct((B, H, S, LANES), BF16),
                   jax.ShapeDtypeStruct((B, H, LANES + SUBLANES, S), BF16)],
        compiler_params=_params(2),
        name="qkv_proj",
    )(h3, kv_nw, a_nw, w_k, w_vt, w_q, cos, sin_lo, sin_hi)


def _diff_attn_kernel(lam_ref, q_ref, qn_ref, k_ref, kn_ref, vt_ref, swt_ref, o_ref,
                      sx_scr, s0_scr, s1_scr, cx_scr, c0_scr, c1_scr, m_scr, acc_scr, *, lam_init):
    bq = q_ref.shape[0]
    bk = bq
    hd = LANES // 2
    i = pl.program_id(2)
    s_bufs = (s0_scr, s1_scr, sx_scr)
    c_bufs = (c0_scr, c1_scr, cx_scr)
    X = 2

    def stacked(q):
        lane = lax.broadcasted_iota(jnp.int32, q.shape, 1)
        zero = jnp.zeros_like(q)
        return jnp.concatenate([jnp.where(lane < hd, q, zero), jnp.where(lane >= hd, q, zero)], axis=0)

    qs = stacked(q_ref[...])

    m_scr[...] = jnp.full_like(m_scr, -jnp.inf)
    acc_scr[...] = jnp.zeros_like(acc_scr)

    def score(k_rows, q_stacked, buf):
        s = _dot_nt(k_rows, q_stacked)
        s_bufs[buf][...] = s
        c_bufs[buf][...] = jnp.max(s, axis=0, keepdims=True)

    def qk(n, buf):
        score(k_ref[pl.ds(pl.multiple_of(n * bk, bk), bk), :], qs, buf)

    def softmax_pv(n, buf, masked):
        s = s_bufs[buf][...]
        m_prev = m_scr[...]
        if masked:
            kk = lax.broadcasted_iota(jnp.int32, (bk, bq), 0)
            qq = lax.broadcasted_iota(jnp.int32, (bk, bq), 1)
            keep = kk <= qq
            s = jnp.where(jnp.concatenate([keep, keep], axis=1), s, -jnp.inf)
            m_new = jnp.maximum(m_prev, jnp.max(s, axis=0, keepdims=True))
        else:
            m_new = jnp.maximum(m_prev, c_bufs[buf][...])
        alpha = jnp.exp2(m_prev - m_new)
        p = jnp.exp2(s - m_new).astype(BF16)
        cols = pl.ds(pl.multiple_of(n * bk, bk), bk)
        acc_scr[...] = alpha * acc_scr[...] + _dot(vt_ref[:, cols], p)
        m_scr[...] = m_new

    @pl.when((pl.program_id(0) == 0) & (pl.program_id(1) == 0) & (i == 0))
    def _():
        qk(0, X)

    @pl.when(i > 0)
    def _():
        qk(1, 1)
        softmax_pv(0, X, False)

    def stage(n, buf):
        qk(n + 1, 1 - buf)
        softmax_pv(n, buf, False)

    def body(n, carry):
        lax.cond(n % 2 == 1, lambda: stage(n, 1), lambda: stage(n, 0))
        return carry

    lax.fori_loop(1, i, body, 0)

    def tail(buf):
        softmax_pv(i, buf, True)
        score(kn_ref[...], stacked(qn_ref[...]), X)
        lp = lam_ref[...]
        lam = (jnp.exp(jnp.sum(lp[0:1, :] * lp[1:2, :], axis=-1, keepdims=True))
               - jnp.exp(jnp.sum(lp[2:3, :] * lp[3:4, :], axis=-1, keepdims=True)) + lam_init)
        acc = acc_scr[...]
        o_all = acc[:LANES, :] / acc[LANES:LANES + 1, :]
        ot = o_all[:, :bq] - lam * o_all[:, bq:]
        ms = jnp.mean(ot * ot, axis=0, keepdims=True)
        ot = ot * lax.rsqrt(ms + EPS) * swt_ref[...] * (1.0 - lam_init)
        o_ref[...] = ot.T.astype(BF16)

    lax.cond(i % 2 == 1, lambda: tail(1), lambda: lax.cond(i > 0, lambda: tail(0), lambda: tail(X)))


def _diff_attn(q4, k4, vt4, lam_p, subln_col, lam_init):
    B, H, S, _ = q4.shape
    bq = ATTN_BLOCK
    nq = S // bq

    def next_step(b, h, i):
        i2 = i + 1
        h2 = h + i2 // nq
        b2 = b + h2 // H
        last = b2 >= B
        return (jnp.where(last, b, b2), jnp.where(last, h, h2 % H), jnp.where(last, i, i2 % nq))

    def q_next(b, h, i):
        b2, h2, i2 = next_step(b, h, i)
        return (b2, h2, i2, 0)

    def k_next(b, h, i):
        b2, h2, _ = next_step(b, h, i)
        return (b2, h2, 0, 0)

    return pl.pallas_call(
        functools.partial(_diff_attn_kernel, lam_init=lam_init),
        grid=(B, H, nq),
        in_specs=[
            _full(lam_p.shape),
            pl.BlockSpec((None, None, bq, LANES), lambda b, h, i: (b, h, i, 0)),
            pl.BlockSpec((None, None, bq, LANES), q_next),
            pl.BlockSpec((None, None, S, LANES), lambda b, h, i: (b, h, 0, 0)),
            pl.BlockSpec((None, None, bq, LANES), k_next),
            pl.BlockSpec((None, None, LANES + SUBLANES, S), lambda b, h, i: (b, h, 0, 0)),
            _full(subln_col.shape),
        ],
        out_specs=pl.BlockSpec((None, None, bq, LANES), lambda b, h, i: (b, h, i, 0)),
        out_shape=jax.ShapeDtypeStruct((B, H, S, LANES), BF16),
        scratch_shapes=[
            pltpu.VMEM((bq, 2 * bq), F32),
            pltpu.VMEM((bq, 2 * bq), F32),
            pltpu.VMEM((bq, 2 * bq), F32),
            pltpu.VMEM((1, 2 * bq), F32),
            pltpu.VMEM((1, 2 * bq), F32),
            pltpu.VMEM((1, 2 * bq), F32),
            pltpu.VMEM((1, 2 * bq), F32),
            pltpu.VMEM((LANES + SUBLANES, 2 * bq), F32),
        ],
        compiler_params=_params(3),
        name="diff_attn",
    )(lam_p, q4, q4, k4, k4, vt4, subln_col)


def _rope_tables(seq, hd):
    pos = jnp.arange(seq, dtype=F32)
    inv = ROPE_THETA ** (-jnp.arange(0, hd, 2, dtype=F32) / hd)
    f = pos[:, None] * inv[None, :]
    cos_h, sin_h = jnp.cos(f), jnp.sin(f)
    zeros = jnp.zeros_like(sin_h)
    cos = jnp.concatenate([cos_h, cos_h] * 2, axis=-1)
    sin_lo = jnp.concatenate([-sin_h, zeros] * 2, axis=-1)
    sin_hi = jnp.concatenate([zeros, sin_h] * 2, axis=-1)
    return cos, sin_lo, sin_hi


def kernel(x, attn_norm_w, ffn_norm_w, gla_w_qkvg, gla_w_gk1, gla_w_gk2, gla_b_gk, gla_onorm_w, gla_w_o, kv_norm_w, w_kv, diff_w_q, diff_lambda, diff_subln_w, diff_w_o, ffn_w_in, ffn_conv_w, ffn_conv_b, ffn_w_out, final_norm_w):
    B, S, D = x.shape
    depth = attn_norm_w.shape[0]
    n_a = gla_w_qkvg.shape[0]
    dk = gla_w_gk2.shape[2]
    dv = gla_w_o.shape[1]
    n_qk = diff_w_q.shape[2]
    hd = n_qk // (2 * DIFF_HEADS)
    assert hd * 2 == LANES and (w_kv.shape[1] - n_qk) == DIFF_HEADS * LANES
    assert S % TOK_BLOCK == 0 and S % ATTN_BLOCK == 0 and TOK_BLOCK % GLA_CHUNK == 0
    assert ffn_w_out.shape[1] % FFN_CHUNK == 0 and GLA_GATE_RANK <= LANES

    row = lambda v: v.reshape(1, -1)
    cos, sin_lo, sin_hi = _rope_tables(S, hd)

    ffn_stacks = (ffn_norm_w[:, None, :], ffn_w_in.astype(BF16), ffn_conv_w, ffn_conv_b[:, None, :],
                  ffn_w_out.astype(BF16))
    h = x
    for l in range(depth):
        final_nw = row(final_norm_w) if l == depth - 1 else None
        ffn_args = (l,) + ffn_stacks
        if l < n_a:
            w1_pad = jnp.pad(gla_w_gk1[l], ((0, 0), (0, LANES - GLA_GATE_RANK)))
            w_cat = jnp.concatenate([gla_w_qkvg[l], w1_pad], axis=1).astype(BF16)
            w2_pad = jnp.pad(gla_w_gk2[l], ((0, LANES - GLA_GATE_RANK), (0, 0))).astype(BF16)
            qkvg, gk = _gla_proj(h.reshape(B * S, D), row(attn_norm_w[l]), w_cat, w2_pad,
                                 row(gla_b_gk[l]))
            h = _gla_core(qkvg.reshape(B, S, -1), gk.reshape(B, S, dk), h, row(gla_onorm_w[l]),
                          gla_w_o[l].astype(BF16), dk, dv)
            h = _ffn(h, *ffn_args, final_nw=final_nw)
        else:
            j = l - n_a
            lam_init = 0.8 - 0.6 * math.exp(-0.3 * l)
            q3, k_new, vt_new = _qkv_proj(h, row(kv_norm_w), row(attn_norm_w[l]),
                                          w_kv[:, :n_qk].astype(BF16), w_kv[:, n_qk:].T.astype(BF16),
                                          diff_w_q[j].astype(BF16), cos, sin_lo, sin_hi,
                                          float(hd) ** -0.5 * LOG2E)
            if l == n_a:
                k3, vt4 = k_new, vt_new
            o3 = _diff_attn(q3, k3, vt4, diff_lambda[j], diff_subln_w[j].reshape(-1, 1), lam_init)
            h = _ffn(h, *ffn_args, mix3=o3, wo=diff_w_o[j].astype(BF16), final_nw=final_nw)
    return h
```

```python
import functools
import math

import jax
import jax.numpy as jnp
from jax import lax
from jax.experimental import pallas as pl
from jax.experimental.pallas import tpu as pltpu

F32 = jnp.float32
BF16 = jnp.bfloat16

EPS = 1e-6
ROPE_THETA = 10000.0
GLA_HEADS = 4
GLA_CHUNK = 64
GLA_GATE_RANK = 16
GLA_GATE_NORM = 16.0
DIFF_HEADS = 8
CONV_WIDTH = 3

LANES = 128
SUBLANES = 8
VMEM_LIMIT = 56 * 1024 * 1024

TOK_BLOCK = 512
FFN_CHUNK = 256
ATTN_BLOCK = 512
LOG2E = math.log2(math.e)


def _params(n_axes):
    return pltpu.CompilerParams(
        dimension_semantics=("arbitrary",) * n_axes, vmem_limit_bytes=VMEM_LIMIT)


def _full(shape):
    nd = len(shape)
    return pl.BlockSpec(shape, lambda *_: (0,) * nd)


def _layer_slab(shape, layer):
    return pl.BlockSpec((None,) + tuple(shape[1:]), lambda *_: (layer, 0, 0), pipeline_mode=pl.Buffered(1))


def _rms(x, w):
    ms = jnp.mean(x * x, axis=-1, keepdims=True)
    return x * lax.rsqrt(ms + EPS) * w


def _sigmoid(x):
    return 1.0 / (1.0 + jnp.exp2(x * (-LOG2E)))


def _dot(a, b):
    return jnp.dot(a, b, preferred_element_type=F32)


def _dot_nt(a, b):
    return lax.dot_general(a, b, (((1,), (1,)), ((), ())), preferred_element_type=F32)


def _dot_tn(a, b):
    return lax.dot_general(a, b, (((0,), (0,)), ((), ())), preferred_element_type=F32)


def _gla_proj_kernel(x_ref, nw_ref, w_ref, w2_ref, b_ref, qkvg_ref, gk_ref, *, n_main):
    hn = _rms(x_ref[...], nw_ref[...]).astype(BF16)
    proj = _dot(hn, w_ref[...])
    qkvg_ref[...] = proj[:, :n_main].astype(BF16)
    low = proj[:, n_main:].astype(BF16)
    z = _dot(low, w2_ref[...]) + b_ref[...]
    gk_ref[...] = (jnp.minimum(z, 0.0) - jnp.log(1.0 + jnp.exp(-jnp.abs(z)))) * (1.0 / GLA_GATE_NORM)


def _gla_proj(x2, nw, w_cat, w2_pad, b_gk):
    T, D = x2.shape
    n_main = w_cat.shape[1] - LANES
    dk = w2_pad.shape[1]
    tm = TOK_BLOCK
    return pl.pallas_call(
        functools.partial(_gla_proj_kernel, n_main=n_main),
        grid=(T // tm,),
        in_specs=[
            pl.BlockSpec((tm, D), lambda i: (i, 0)),
            _full(nw.shape), _full(w_cat.shape), _full(w2_pad.shape), _full(b_gk.shape),
        ],
        out_specs=[
            pl.BlockSpec((tm, n_main), lambda i: (i, 0)),
            pl.BlockSpec((tm, dk), lambda i: (i, 0)),
        ],
        out_shape=[
            jax.ShapeDtypeStruct((T, n_main), BF16),
            jax.ShapeDtypeStruct((T, dk), F32),
        ],
        compiler_params=_params(1),
        name="gla_proj",
    )(x2, nw, w_cat, w2_pad, b_gk)


def _cumsum_rows(g):
    n = g.shape[0]
    ridx = lax.broadcasted_iota(jnp.int32, g.shape, 0)
    b = g
    s = 1
    while s < n:
        b = b + jnp.where(ridx >= s, pltpu.roll(b, s, axis=0), 0.0)
        s *= 2
    return b


def _gla_core_kernel(qkvg_ref, gk_ref, x_ref, onw_ref, wo_ref, out_ref, state_ref, o_scr,
                     *, dk, dv):
    C = GLA_CHUNK
    H = GLA_HEADS
    hk = dk // H
    hv = dv // H
    tb = x_ref.shape[0]

    @pl.when(pl.program_id(1) == 0)
    def _():
        state_ref[...] = jnp.zeros_like(state_ref)

    row = lax.broadcasted_iota(jnp.int32, (C, C), 0)
    col = lax.broadcasted_iota(jnp.int32, (C, C), 1)
    causal = col <= row
    qscale = hk ** -0.5

    for c in range(tb // C):
        r = pl.ds(c * C, C)
        b_all = _cumsum_rows(gk_ref[r, :])
        for h in range(H):
            q = qkvg_ref[r, h * hk:(h + 1) * hk].astype(F32)
            k = qkvg_ref[r, dk + h * hk:dk + (h + 1) * hk].astype(F32)
            v = qkvg_ref[r, 2 * dk + h * hv:2 * dk + (h + 1) * hv]
            b = b_all[:, h * hk:(h + 1) * hk]
            b_last = b[C - 1:C, :]
            q_in = (q * qscale * jnp.exp(b)).astype(BF16)
            k_in = (k * jnp.exp(-b)).astype(BF16)
            k_end = (k * jnp.exp(b_last - b)).astype(BF16)
            scores = jnp.where(causal, _dot_nt(q_in, k_in), 0.0)
            st = state_ref[h]
            o = _dot(scores.astype(BF16), v) + _dot_nt(q_in, st.astype(BF16))
            o_scr[r, h * hv:(h + 1) * hv] = o
            state_ref[h] = st * jnp.exp(b_last) + _dot_tn(v, k_end)

    onw = onw_ref[...]
    parts = []
    for h in range(H):
        parts.append(_rms(o_scr[:, h * hv:(h + 1) * hv], onw))
    on = jnp.concatenate(parts, axis=1)
    g = qkvg_ref[:, 2 * dk + dv:2 * dk + 2 * dv].astype(F32)
    gated = (on * (g * _sigmoid(g))).astype(BF16)
    out_ref[...] = x_ref[...] + _dot(gated, wo_ref[...])


def _gla_core(qkvg3, gk3, x3, onw, wo, dk, dv):
    B, S, D = x3.shape
    tb = TOK_BLOCK
    H = GLA_HEADS
    return pl.pallas_call(
        functools.partial(_gla_core_kernel, dk=dk, dv=dv),
        grid=(B, S // tb),
        in_specs=[
            pl.BlockSpec((None, tb, qkvg3.shape[2]), lambda b, s: (b, s, 0)),
            pl.BlockSpec((None, tb, dk), lambda b, s: (b, s, 0)),
            pl.BlockSpec((None, tb, D), lambda b, s: (b, s, 0)),
            _full(onw.shape), _full(wo.shape),
        ],
        out_specs=pl.BlockSpec((None, tb, D), lambda b, s: (b, s, 0)),
        out_shape=jax.ShapeDtypeStruct((B, S, D), F32),
        scratch_shapes=[
            pltpu.VMEM((H, dv // H, dk // H), F32),
            pltpu.VMEM((tb, dv), F32),
        ],
        compiler_params=_params(2),
        name="gla_core",
    )(qkvg3, gk3, x3, onw, wo)


def _ffn_kernel(*refs, has_mix, final_norm, d_ff):
    refs = list(refs)
    h_ref = refs.pop(0)
    if has_mix:
        mix_ref = refs.pop(0)
        wo_ref = refs.pop(0)
    nw_ref, win_ref, cw_ref, cb_ref, wout_ref = refs[:5]
    refs = refs[5:]
    if final_norm:
        fnw_ref = refs.pop(0)
    out_ref, perm_ref, u_ref, carry_ref, act_ref = refs

    tm, d_model = h_ref.shape
    fc = FFN_CHUNK
    sub = SUBLANES
    seg = tm // sub
    pitch = seg + sub
    n_slabs = d_model // LANES

    @pl.when(pl.program_id(1) == 0)
    def _():
        carry_ref[...] = jnp.zeros_like(carry_ref)

    h = h_ref[...]
    if has_mix:
        mix = jnp.concatenate([mix_ref[i] for i in range(mix_ref.shape[0])], axis=1)
        h = h + _dot(mix, wo_ref[...])

    def interleave(x):
        for c in range(n_slabs):
            for s_ in range(sub):
                perm_ref[c, pl.ds(s_ * pitch, seg), :] = x[s_ * seg:(s_ + 1) * seg, c * LANES:(c + 1) * LANES]
        cols = []
        for c in range(n_slabs):
            cols.append(jnp.concatenate(
                [perm_ref[c, pl.ds(j, sub, stride=pitch), :] for j in range(seg)], axis=0))
        return jnp.concatenate(cols, axis=1)

    def deinterleave_store(y):
        for c in range(n_slabs):
            for j in range(seg):
                perm_ref[c, pl.ds(j, sub, stride=pitch), :] = y[j * sub:(j + 1) * sub, c * LANES:(c + 1) * LANES]
        for c in range(n_slabs):
            for s_ in range(sub):
                out_ref[s_ * seg:(s_ + 1) * seg, c * LANES:(c + 1) * LANES] = perm_ref[c, pl.ds(s_ * pitch, seg), :]

    h = interleave(h)
    hn = _rms(h, nw_ref[...]).astype(BF16)
    first_row = lax.broadcasted_iota(jnp.int32, (sub, fc), 0) == 0

    def up(j):
        for part in range(2):
            u_ref[2 * j + part] = _dot(hn, win_ref[:, pl.ds(part * d_ff + j * fc, fc)])

    def shift(y, e, k):
        top = pltpu.roll(y[tm - sub:, :], 1, axis=0)
        fixed = jnp.where(first_row, carry_ref[e, k], top)
        carry_ref[e, k] = top
        return jnp.concatenate([fixed, y[:tm - sub, :]], axis=0)

    def conv(j, part):
        e = 2 * j + part
        cols = pl.ds(part * d_ff + j * fc, fc)
        u = u_ref[e]
        cw = cw_ref[:, cols]
        y = shift(u * cw[0:1, :], e, 0) + u * cw[1:2, :]
        return shift(y, e, 1) + u * cw[2:3, :] + cb_ref[:, cols]

    n_chunks = d_ff // fc
    up(0)
    for j in range(n_chunks):
        if j + 1 < n_chunks:
            up(j + 1)
        a = conv(j, 0)
        g = conv(j, 1)
        act_ref[:, pl.ds(j * fc, fc)] = (g * _sigmoid(g) * a).astype(BF16)
    out = h + _dot(act_ref[...], wout_ref[...])
    if final_norm:
        out = _rms(out, fnw_ref[...])
    deinterleave_store(out)


def _ffn(h3, layer, nw, w_in, conv_w, conv_b, w_out, mix3=None, wo=None, final_nw=None):
    B, S, D = h3.shape
    tm = TOK_BLOCK
    d_ff = w_out.shape[1]
    has_mix = mix3 is not None
    final_norm = final_nw is not None
    tok = pl.BlockSpec((None, tm, D), lambda b, s: (b, s, 0))
    args, specs = [h3], [tok]
    if has_mix:
        args += [mix3, wo]
        specs += [pl.BlockSpec((None, mix3.shape[1], tm, mix3.shape[3]), lambda b, s: (b, 0, s, 0)),
                  _full(wo.shape)]
    args += [nw, w_in, conv_w, conv_b, w_out]
    specs += [_layer_slab(a.shape, layer) for a in (nw, w_in, conv_w, conv_b, w_out)]
    if final_norm:
        args.append(final_nw)
        specs.append(_full(final_nw.shape))
    n_chunks = d_ff // FFN_CHUNK
    return pl.pallas_call(
        functools.partial(_ffn_kernel, has_mix=has_mix, final_norm=final_norm, d_ff=d_ff),
        grid=(B, S // tm),
        in_specs=specs,
        out_specs=tok,
        out_shape=jax.ShapeDtypeStruct((B, S, D), F32),
        scratch_shapes=[pltpu.VMEM((D // LANES, tm + SUBLANES * SUBLANES, LANES), F32),
                        pltpu.VMEM((2 * n_chunks, tm, FFN_CHUNK), F32),
                        pltpu.VMEM((2 * n_chunks, 2, SUBLANES, FFN_CHUNK), F32),
                        pltpu.VMEM((tm, d_ff), BF16)],
        compiler_params=_params(2),
        name="ffn_mix" if has_mix else "ffn",
    )(*args)


def _rope(x, cos, sin_lo, sin_hi):
    half = LANES // 4
    return (x * cos + pltpu.roll(x, LANES - half, axis=1) * sin_lo
            + pltpu.roll(x, half, axis=1) * sin_hi)


def _qkv_proj_kernel(h_ref, kvnw_ref, anw_ref, wk_ref, wvt_ref, wq_ref, cos_ref, slo_ref, shi_ref,
                     q_ref, k_ref, vt_ref, *, q_scale):
    x = h_ref[...]
    ms = jnp.mean(x * x, axis=-1, keepdims=True)
    xn = x * lax.rsqrt(ms + EPS)
    kvn = (xn * kvnw_ref[...]).astype(BF16)
    an = (xn * anw_ref[...]).astype(BF16)
    k = _dot(kvn, wk_ref[...])
    q = _dot(an, wq_ref[...])
    vt = _dot_nt(wvt_ref[...], kvn)
    cos = cos_ref[...]
    slo = slo_ref[...]
    shi = shi_ref[...]
    for j in range(q.shape[1] // LANES):
        cols = slice(j * LANES, (j + 1) * LANES)
        q_ref[j] = (_rope(q[:, cols], cos, slo, shi) * q_scale).astype(BF16)
        k_ref[j] = _rope(k[:, cols], cos, slo, shi).astype(BF16)
    ones = jnp.ones((SUBLANES, vt.shape[1]), BF16)
    for h in range(vt_ref.shape[0]):
        vt_ref[h, :LANES, :] = vt[h * LANES:(h + 1) * LANES, :].astype(BF16)
        vt_ref[h, LANES:, :] = ones


def _qkv_proj(h3, kv_nw, a_nw, w_k, w_vt, w_q, cos, sin_lo, sin_hi, q_scale):
    B, S, D = h3.shape
    tm = TOK_BLOCK
    n_qk = w_q.shape[1]
    H = w_vt.shape[0] // LANES
    tok = lambda n: pl.BlockSpec((None, tm, n), lambda b, s: (b, s, 0))
    tab = pl.BlockSpec((tm, LANES), lambda b, s: (s, 0))
    return pl.pallas_call(
        functools.partial(_qkv_proj_kernel, q_scale=q_scale),
        grid=(B, S // tm),
        in_specs=[tok(D), _full(kv_nw.shape), _full(a_nw.shape), _full(w_k.shape), _full(w_vt.shape),
                  _full(w_q.shape), tab, tab, tab],
        out_specs=[pl.BlockSpec((None, H, tm, LANES), lambda b, s: (b, 0, s, 0)),
                   pl.BlockSpec((None, H, tm, LANES), lambda b, s: (b, 0, s, 0)),
                   pl.BlockSpec((None, H, LANES + SUBLANES, tm), lambda b, s: (b, 0, 0, s))],
        out_shape=[jax.ShapeDtypeStruct((B, H, S, LANES), BF16),
                   jax.ShapeDtypeStruct((B, H, S, LANES), BF16),
                   jax.ShapeDtypeStruct((B, H, LANES + SUBLANES, S), BF16)],
        compiler_params=_params(2),
        name="qkv_proj",
    )(h3, kv_nw, a_nw, w_k, w_vt, w_q, cos, sin_lo, sin_hi)


def _diff_attn_kernel(lam_ref, q_ref, qn_ref, k_ref, kn_ref, vt_ref, swt_ref, o_ref,
                      sx_scr, s0_scr, s1_scr, cx_scr, c0_scr, c1_scr, m_scr, acc_scr, *, lam_init, nq):
    bq = q_ref.shape[0]
    bk = bq
    hd = LANES // 2
    i = pl.program_id(2)
    s_bufs = (s0_scr, s1_scr, sx_scr)
    c_bufs = (c0_scr, c1_scr, cx_scr)
    X = 2

    def stacked(q):
        lane = lax.broadcasted_iota(jnp.int32, q.shape, 1)
        zero = jnp.zeros_like(q)
        return jnp.concatenate([jnp.where(lane < hd, q, zero), jnp.where(lane >= hd, q, zero)], axis=0)

    qs = stacked(q_ref[...])

    m_scr[...] = jnp.full_like(m_scr, -jnp.inf)
    acc_scr[...] = jnp.zeros_like(acc_scr)

    def score(k_rows, q_stacked, buf):
        s = _dot_nt(k_rows, q_stacked)
        s_bufs[buf][...] = s
        c_bufs[buf][...] = jnp.max(s, axis=0, keepdims=True)

    def qk(n, buf):
        score(k_ref[pl.ds(n * bk, bk), :], qs, buf)

    def softmax_pv(n, buf, masked):
        s = s_bufs[buf][...]
        m_prev = m_scr[...]
        if masked:
            kk = lax.broadcasted_iota(jnp.int32, (bk, bq), 0)
            qq = lax.broadcasted_iota(jnp.int32, (bk, bq), 1)
            keep = kk <= qq
            s = jnp.where(jnp.concatenate([keep, keep], axis=1), s, -jnp.inf)
            m_new = jnp.maximum(m_prev, jnp.max(s, axis=0, keepdims=True))
        else:
            m_new = jnp.maximum(m_prev, c_bufs[buf][...])
        alpha = jnp.exp2(m_prev - m_new)
        p = jnp.exp2(s - m_new).astype(BF16)
        acc_scr[...] = alpha * acc_scr[...] + _dot(vt_ref[:, pl.ds(n * bk, bk)], p)
        m_scr[...] = m_new

    @pl.when((pl.program_id(0) == 0) & (pl.program_id(1) == 0) & (i == 0))
    def _():
        qk(0, X)

    def finish(ii, buf):
        softmax_pv(ii, buf, True)
        score(kn_ref[...], stacked(qn_ref[...]), X)
        lp = lam_ref[...]
        lam = (jnp.exp(jnp.sum(lp[0:1, :] * lp[1:2, :], axis=-1, keepdims=True))
               - jnp.exp(jnp.sum(lp[2:3, :] * lp[3:4, :], axis=-1, keepdims=True)) + lam_init)
        acc = acc_scr[...]
        o_all = acc[:LANES, :] / acc[LANES:LANES + 1, :]
        ot = o_all[:, :bq] - lam * o_all[:, bq:]
        ms = jnp.mean(ot * ot, axis=0, keepdims=True)
        ot = ot * lax.rsqrt(ms + EPS) * swt_ref[...] * (1.0 - lam_init)
        o_ref[...] = ot.T.astype(BF16)

    def whole_step(ii):
        def run():
            if ii == 0:
                finish(0, X)
                return
            qk(1, 1)
            softmax_pv(0, X, False)
            for n in range(1, ii):
                qk(n + 1, (n + 1) % 2)
                softmax_pv(n, n % 2, False)
            finish(ii, ii % 2)
        return run

    lax.switch(i, [whole_step(ii) for ii in range(nq)])


def _diff_attn(q4, k4, vt4, lam_p, subln_col, lam_init):
    B, H, S, _ = q4.shape
    bq = ATTN_BLOCK
    nq = S // bq

    def next_step(b, h, i):
        i2 = i + 1
        h2 = h + i2 // nq
        b2 = b + h2 // H
        last = b2 >= B
        return (jnp.where(last, b, b2), jnp.where(last, h, h2 % H), jnp.where(last, i, i2 % nq))

    def q_next(b, h, i):
        b2, h2, i2 = next_step(b, h, i)
        return (b2, h2, i2, 0)

    def k_next(b, h, i):
        b2, h2, _ = next_step(b, h, i)
        return (b2, h2, 0, 0)

    return pl.pallas_call(
        functools.partial(_diff_attn_kernel, lam_init=lam_init, nq=nq),
        grid=(B, H, nq),
        in_specs=[
            _full(lam_p.shape),
            pl.BlockSpec((None, None, bq, LANES), lambda b, h, i: (b, h, i, 0)),
            pl.BlockSpec((None, None, bq, LANES), q_next),
            pl.BlockSpec((None, None, S, LANES), lambda b, h, i: (b, h, 0, 0)),
            pl.BlockSpec((None, None, bq, LANES), k_next),
            pl.BlockSpec((None, None, LANES + SUBLANES, S), lambda b, h, i: (b, h, 0, 0)),
            _full(subln_col.shape),
        ],
        out_specs=pl.BlockSpec((None, None, bq, LANES), lambda b, h, i: (b, h, i, 0)),
        out_shape=jax.ShapeDtypeStruct((B, H, S, LANES), BF16),
        scratch_shapes=[
            pltpu.VMEM((bq, 2 * bq), F32),
            pltpu.VMEM((bq, 2 * bq), F32),
            pltpu.VMEM((bq, 2 * bq), F32),
            pltpu.VMEM((1, 2 * bq), F32),
            pltpu.VMEM((1, 2 * bq), F32),
            pltpu.VMEM((1, 2 * bq), F32),
            pltpu.VMEM((1, 2 * bq), F32),
            pltpu.VMEM((LANES + SUBLANES, 2 * bq), F32),
        ],
        compiler_params=_params(3),
        name="diff_attn",
    )(lam_p, q4, q4, k4, k4, vt4, subln_col)


def _rope_tables(seq, hd):
    pos = jnp.arange(seq, dtype=F32)
    inv = ROPE_THETA ** (-jnp.arange(0, hd, 2, dtype=F32) / hd)
    f = pos[:, None] * inv[None, :]
    cos_h, sin_h = jnp.cos(f), jnp.sin(f)
    zeros = jnp.zeros_like(sin_h)
    cos = jnp.concatenate([cos_h, cos_h] * 2, axis=-1)
    sin_lo = jnp.concatenate([-sin_h, zeros] * 2, axis=-1)
    sin_hi = jnp.concatenate([zeros, sin_h] * 2, axis=-1)
    return cos, sin_lo, sin_hi


def kernel(x, attn_norm_w, ffn_norm_w, gla_w_qkvg, gla_w_gk1, gla_w_gk2, gla_b_gk, gla_onorm_w, gla_w_o, kv_norm_w, w_kv, diff_w_q, diff_lambda, diff_subln_w, diff_w_o, ffn_w_in, ffn_conv_w, ffn_conv_b, ffn_w_out, final_norm_w):
    B, S, D = x.shape
    depth = attn_norm_w.shape[0]
    n_a = gla_w_qkvg.shape[0]
    dk = gla_w_gk2.shape[2]
    dv = gla_w_o.shape[1]
    n_qk = diff_w_q.shape[2]
    hd = n_qk // (2 * DIFF_HEADS)
    assert hd * 2 == LANES and (w_kv.shape[1] - n_qk) == DIFF_HEADS * LANES
    assert S % TOK_BLOCK == 0 and S % ATTN_BLOCK == 0 and TOK_BLOCK % GLA_CHUNK == 0
    assert ffn_w_out.shape[1] % FFN_CHUNK == 0 and GLA_GATE_RANK <= LANES

    row = lambda v: v.reshape(1, -1)
    cos, sin_lo, sin_hi = _rope_tables(S, hd)

    ffn_stacks = (ffn_norm_w[:, None, :], ffn_w_in.astype(BF16), ffn_conv_w, ffn_conv_b[:, None, :],
                  ffn_w_out.astype(BF16))
    h = x
    for l in range(depth):
        final_nw = row(final_norm_w) if l == depth - 1 else None
        ffn_args = (l,) + ffn_stacks
        if l < n_a:
            w1_pad = jnp.pad(gla_w_gk1[l], ((0, 0), (0, LANES - GLA_GATE_RANK)))
            w_cat = jnp.concatenate([gla_w_qkvg[l], w1_pad], axis=1).astype(BF16)
            w2_pad = jnp.pad(gla_w_gk2[l], ((0, LANES - GLA_GATE_RANK), (0, 0))).astype(BF16)
            qkvg, gk = _gla_proj(h.reshape(B * S, D), row(attn_norm_w[l]), w_cat, w2_pad,
                                 row(gla_b_gk[l]))
            h = _gla_core(qkvg.reshape(B, S, -1), gk.reshape(B, S, dk), h, row(gla_onorm_w[l]),
                          gla_w_o[l].astype(BF16), dk, dv)
            h = _ffn(h, *ffn_args, final_nw=final_nw)
        else:
            j = l - n_a
            lam_init = 0.8 - 0.6 * math.exp(-0.3 * l)
            q3, k_new, vt_new = _qkv_proj(h, row(kv_norm_w), row(attn_norm_w[l]),
                                          w_kv[:, :n_qk].astype(BF16), w_kv[:, n_qk:].T.astype(BF16),
                                          diff_w_q[j].astype(BF16), cos, sin_lo, sin_hi,
                                          float(hd) ** -0.5 * LOG2E)
            if l == n_a:
                k3, vt4 = k_new, vt_new
            o3 = _diff_attn(q3, k3, vt4, diff_lambda[j], diff_subln_w[j].reshape(-1, 1), lam_init)
            h = _ffn(h, *ffn_args, mix3=o3, wo=diff_w_o[j].astype(BF16), final_nw=final_nw)
    return h
```

```python
import functools
import math

import jax
import jax.numpy as jnp
from jax import lax
from jax.experimental import pallas as pl
from jax.experimental.pallas import tpu as pltpu

F32 = jnp.float32
BF16 = jnp.bfloat16

EPS = 1e-6
ROPE_THETA = 10000.0
GLA_HEADS = 4
GLA_CHUNK = 64
GLA_GATE_RANK = 16
GLA_GATE_NORM = 16.0
DIFF_HEADS = 8
CONV_WIDTH = 3

LANES = 128
SUBLANES = 8
VMEM_LIMIT = 56 * 1024 * 1024

TOK_BLOCK = 512
FFN_CHUNK = 256
ATTN_BLOCK = 512
ATTN_GROUP = 2
LOG2E = math.log2(math.e)


def _params(n_axes):
    return pltpu.CompilerParams(
        dimension_semantics=("arbitrary",) * n_axes, vmem_limit_bytes=VMEM_LIMIT)


def _full(shape):
    nd = len(shape)
    return pl.BlockSpec(shape, lambda *_: (0,) * nd)


def _layer_slab(shape, layer):
    return pl.BlockSpec((None,) + tuple(shape[1:]), lambda *_: (layer, 0, 0), pipeline_mode=pl.Buffered(1))


def _rms(x, w):
    ms = jnp.mean(x * x, axis=-1, keepdims=True)
    return x * lax.rsqrt(ms + EPS) * w


def _sigmoid(x):
    return 1.0 / (1.0 + jnp.exp2(x * (-LOG2E)))


def _dot(a, b):
    return jnp.dot(a, b, preferred_element_type=F32)


def _dot_nt(a, b):
    return lax.dot_general(a, b, (((1,), (1,)), ((), ())), preferred_element_type=F32)


def _dot_tn(a, b):
    return lax.dot_general(a, b, (((0,), (0,)), ((), ())), preferred_element_type=F32)


def _gla_proj_kernel(x_ref, nw_ref, w_ref, w2_ref, b_ref, qkvg_ref, gk_ref, *, n_main):
    hn = _rms(x_ref[...], nw_ref[...]).astype(BF16)
    proj = _dot(hn, w_ref[...])
    qkvg_ref[...] = proj[:, :n_main].astype(BF16)
    low = proj[:, n_main:].astype(BF16)
    z = _dot(low, w2_ref[...]) + b_ref[...]
    gk_ref[...] = (jnp.minimum(z, 0.0) - jnp.log(1.0 + jnp.exp(-jnp.abs(z)))) * (1.0 / GLA_GATE_NORM)


def _gla_proj(x2, nw, w_cat, w2_pad, b_gk):
    T, D = x2.shape
    n_main = w_cat.shape[1] - LANES
    dk = w2_pad.shape[1]
    tm = TOK_BLOCK
    return pl.pallas_call(
        functools.partial(_gla_proj_kernel, n_main=n_main),
        grid=(T // tm,),
        in_specs=[
            pl.BlockSpec((tm, D), lambda i: (i, 0)),
            _full(nw.shape), _full(w_cat.shape), _full(w2_pad.shape), _full(b_gk.shape),
        ],
        out_specs=[
            pl.BlockSpec((tm, n_main), lambda i: (i, 0)),
            pl.BlockSpec((tm, dk), lambda i: (i, 0)),
        ],
        out_shape=[
            jax.ShapeDtypeStruct((T, n_main), BF16),
            jax.ShapeDtypeStruct((T, dk), F32),
        ],
        compiler_params=_params(1),
        name="gla_proj",
    )(x2, nw, w_cat, w2_pad, b_gk)


def _cumsum_rows(g):
    n = g.shape[0]
    ridx = lax.broadcasted_iota(jnp.int32, g.shape, 0)
    b = g
    s = 1
    while s < n:
        b = b + jnp.where(ridx >= s, pltpu.roll(b, s, axis=0), 0.0)
        s *= 2
    return b


def _gla_core_kernel(qkvg_ref, gk_ref, x_ref, onw_ref, wo_ref, out_ref, state_ref, o_scr,
                     *, dk, dv):
    C = GLA_CHUNK
    H = GLA_HEADS
    hk = dk // H
    hv = dv // H
    tb = x_ref.shape[0]

    @pl.when(pl.program_id(1) == 0)
    def _():
        state_ref[...] = jnp.zeros_like(state_ref)

    row = lax.broadcasted_iota(jnp.int32, (C, C), 0)
    col = lax.broadcasted_iota(jnp.int32, (C, C), 1)
    causal = col <= row
    qscale = hk ** -0.5

    for c in range(tb // C):
        r = pl.ds(c * C, C)
        b_all = _cumsum_rows(gk_ref[r, :])
        for h in range(H):
            q = qkvg_ref[r, h * hk:(h + 1) * hk].astype(F32)
            k = qkvg_ref[r, dk + h * hk:dk + (h + 1) * hk].astype(F32)
            v = qkvg_ref[r, 2 * dk + h * hv:2 * dk + (h + 1) * hv]
            b = b_all[:, h * hk:(h + 1) * hk]
            b_last = b[C - 1:C, :]
            q_in = (q * qscale * jnp.exp(b)).astype(BF16)
            k_in = (k * jnp.exp(-b)).astype(BF16)
            k_end = (k * jnp.exp(b_last - b)).astype(BF16)
            scores = jnp.where(causal, _dot_nt(q_in, k_in), 0.0)
            st = state_ref[h]
            o = _dot(scores.astype(BF16), v) + _dot_nt(q_in, st.astype(BF16))
            o_scr[r, h * hv:(h + 1) * hv] = o
            state_ref[h] = st * jnp.exp(b_last) + _dot_tn(v, k_end)

    onw = onw_ref[...]
    parts = []
    for h in range(H):
        parts.append(_rms(o_scr[:, h * hv:(h + 1) * hv], onw))
    on = jnp.concatenate(parts, axis=1)
    g = qkvg_ref[:, 2 * dk + dv:2 * dk + 2 * dv].astype(F32)
    gated = (on * (g * _sigmoid(g))).astype(BF16)
    out_ref[...] = x_ref[...] + _dot(gated, wo_ref[...])


def _gla_core(qkvg3, gk3, x3, onw, wo, dk, dv):
    B, S, D = x3.shape
    tb = TOK_BLOCK
    H = GLA_HEADS
    return pl.pallas_call(
        functools.partial(_gla_core_kernel, dk=dk, dv=dv),
        grid=(B, S // tb),
        in_specs=[
            pl.BlockSpec((None, tb, qkvg3.shape[2]), lambda b, s: (b, s, 0)),
            pl.BlockSpec((None, tb, dk), lambda b, s: (b, s, 0)),
            pl.BlockSpec((None, tb, D), lambda b, s: (b, s, 0)),
            _full(onw.shape), _full(wo.shape),
        ],
        out_specs=pl.BlockSpec((None, tb, D), lambda b, s: (b, s, 0)),
        out_shape=jax.ShapeDtypeStruct((B, S, D), F32),
        scratch_shapes=[
            pltpu.VMEM((H, dv // H, dk // H), F32),
            pltpu.VMEM((tb, dv), F32),
        ],
        compiler_params=_params(2),
        name="gla_core",
    )(qkvg3, gk3, x3, onw, wo)


def _ffn_kernel(*refs, has_mix, final_norm, d_ff):
    refs = list(refs)
    h_ref = refs.pop(0)
    if has_mix:
        mix_ref = refs.pop(0)
        wo_ref = refs.pop(0)
    nw_ref, win_ref, cw_ref, cb_ref, wout_ref = refs[:5]
    refs = refs[5:]
    if final_norm:
        fnw_ref = refs.pop(0)
    out_ref, perm_ref, u_ref, carry_ref, act_ref = refs

    tm, d_model = h_ref.shape
    fc = FFN_CHUNK
    sub = SUBLANES
    seg = tm // sub
    pitch = seg + sub
    n_slabs = d_model // LANES

    @pl.when(pl.program_id(1) == 0)
    def _():
        carry_ref[...] = jnp.zeros_like(carry_ref)

    h = h_ref[...]
    if has_mix:
        mix = jnp.concatenate([mix_ref[i] for i in range(mix_ref.shape[0])], axis=1)
        h = h + _dot(mix, wo_ref[...])

    def interleave(x):
        for c in range(n_slabs):
            for s_ in range(sub):
                perm_ref[c, pl.ds(s_ * pitch, seg), :] = x[s_ * seg:(s_ + 1) * seg, c * LANES:(c + 1) * LANES]
        cols = []
        for c in range(n_slabs):
            cols.append(jnp.concatenate(
                [perm_ref[c, pl.ds(j, sub, stride=pitch), :] for j in range(seg)], axis=0))
        return jnp.concatenate(cols, axis=1)

    def deinterleave_store(y):
        for c in range(n_slabs):
            for j in range(seg):
                perm_ref[c, pl.ds(j, sub, stride=pitch), :] = y[j * sub:(j + 1) * sub, c * LANES:(c + 1) * LANES]
        for c in range(n_slabs):
            for s_ in range(sub):
                out_ref[s_ * seg:(s_ + 1) * seg, c * LANES:(c + 1) * LANES] = perm_ref[c, pl.ds(s_ * pitch, seg), :]

    h = interleave(h)
    hn = _rms(h, nw_ref[...]).astype(BF16)
    first_row = lax.broadcasted_iota(jnp.int32, (sub, fc), 0) == 0

    def up(j):
        for part in range(2):
            u_ref[2 * j + part] = _dot(hn, win_ref[:, pl.ds(part * d_ff + j * fc, fc)])

    def shift(y, e, k):
        top = pltpu.roll(y[tm - sub:, :], 1, axis=0)
        fixed = jnp.where(first_row, carry_ref[e, k], top)
        carry_ref[e, k] = top
        return jnp.concatenate([fixed, y[:tm - sub, :]], axis=0)

    def conv(j, part):
        e = 2 * j + part
        cols = pl.ds(part * d_ff + j * fc, fc)
        u = u_ref[e]
        cw = cw_ref[:, cols]
        y = shift(u * cw[0:1, :], e, 0) + u * cw[1:2, :]
        return shift(y, e, 1) + u * cw[2:3, :] + cb_ref[:, cols]

    n_chunks = d_ff // fc
    up(0)
    for j in range(n_chunks):
        if j + 1 < n_chunks:
            up(j + 1)
        a = conv(j, 0)
        g = conv(j, 1)
        act_ref[:, pl.ds(j * fc, fc)] = (g * _sigmoid(g) * a).astype(BF16)
    out = h + _dot(act_ref[...], wout_ref[...])
    if final_norm:
        out = _rms(out, fnw_ref[...])
    deinterleave_store(out)


def _ffn(h3, layer, nw, w_in, conv_w, conv_b, w_out, mix3=None, wo=None, final_nw=None):
    B, S, D = h3.shape
    tm = TOK_BLOCK
    d_ff = w_out.shape[1]
    has_mix = mix3 is not None
    final_norm = final_nw is not None
    tok = pl.BlockSpec((None, tm, D), lambda b, s: (b, s, 0))
    args, specs = [h3], [tok]
    if has_mix:
        args += [mix3, wo]
        specs += [pl.BlockSpec((None, mix3.shape[1], tm, mix3.shape[3]), lambda b, s: (b, 0, s, 0)),
                  _full(wo.shape)]
    args += [nw, w_in, conv_w, conv_b, w_out]
    specs += [_layer_slab(a.shape, layer) for a in (nw, w_in, conv_w, conv_b, w_out)]
    if final_norm:
        args.append(final_nw)
        specs.append(_full(final_nw.shape))
    n_chunks = d_ff // FFN_CHUNK
    return pl.pallas_call(
        functools.partial(_ffn_kernel, has_mix=has_mix, final_norm=final_norm, d_ff=d_ff),
        grid=(B, S // tm),
        in_specs=specs,
        out_specs=tok,
        out_shape=jax.ShapeDtypeStruct((B, S, D), F32),
        scratch_shapes=[pltpu.VMEM((D // LANES, tm + SUBLANES * SUBLANES, LANES), F32),
                        pltpu.VMEM((2 * n_chunks, tm, FFN_CHUNK), F32),
                        pltpu.VMEM((2 * n_chunks, 2, SUBLANES, FFN_CHUNK), F32),
                        pltpu.VMEM((tm, d_ff), BF16)],
        compiler_params=_params(2),
        name="ffn_mix" if has_mix else "ffn",
    )(*args)


def _rope(x, cos, sin_lo, sin_hi):
    half = LANES // 4
    return (x * cos + pltpu.roll(x, LANES - half, axis=1) * sin_lo
            + pltpu.roll(x, half, axis=1) * sin_hi)


def _qkv_proj_kernel(h_ref, kvnw_ref, anw_ref, wk_ref, wvt_ref, wq_ref, cos_ref, slo_ref, shi_ref,
                     q_ref, k_ref, vt_ref, *, q_scale):
    x = h_ref[...]
    ms = jnp.mean(x * x, axis=-1, keepdims=True)
    xn = x * lax.rsqrt(ms + EPS)
    kvn = (xn * kvnw_ref[...]).astype(BF16)
    an = (xn * anw_ref[...]).astype(BF16)
    k = _dot(kvn, wk_ref[...])
    q = _dot(an, wq_ref[...])
    vt = _dot_nt(wvt_ref[...], kvn)
    cos = cos_ref[...]
    slo = slo_ref[...]
    shi = shi_ref[...]
    for j in range(q.shape[1] // LANES):
        cols = slice(j * LANES, (j + 1) * LANES)
        q_ref[j] = (_rope(q[:, cols], cos, slo, shi) * q_scale).astype(BF16)
        k_ref[j] = _rope(k[:, cols], cos, slo, shi).astype(BF16)
    ones = jnp.ones((SUBLANES, vt.shape[1]), BF16)
    for h in range(vt_ref.shape[0]):
        vt_ref[h, :LANES, :] = vt[h * LANES:(h + 1) * LANES, :].astype(BF16)
        vt_ref[h, LANES:, :] = ones


def _qkv_proj(h3, kv_nw, a_nw, w_k, w_vt, w_q, cos, sin_lo, sin_hi, q_scale):
    B, S, D = h3.shape
    tm = TOK_BLOCK
    n_qk = w_q.shape[1]
    H = w_vt.shape[0] // LANES
    tok = lambda n: pl.BlockSpec((None, tm, n), lambda b, s: (b, s, 0))
    tab = pl.BlockSpec((tm, LANES), lambda b, s: (s, 0))
    return pl.pallas_call(
        functools.partial(_qkv_proj_kernel, q_scale=q_scale),
        grid=(B, S // tm),
        in_specs=[tok(D), _full(kv_nw.shape), _full(a_nw.shape), _full(w_k.shape), _full(w_vt.shape),
                  _full(w_q.shape), tab, tab, tab],
        out_specs=[pl.BlockSpec((None, H, tm, LANES), lambda b, s: (b, 0, s, 0)),
                   pl.BlockSpec((None, H, tm, LANES), lambda b, s: (b, 0, s, 0)),
                   pl.BlockSpec((None, H, LANES + SUBLANES, tm), lambda b, s: (b, 0, 0, s))],
        out_shape=[jax.ShapeDtypeStruct((B, H, S, LANES), BF16),
                   jax.ShapeDtypeStruct((B, H, S, LANES), BF16),
                   jax.ShapeDtypeStruct((B, H, LANES + SUBLANES, S), BF16)],
        compiler_params=_params(2),
        name="qkv_proj",
    )(h3, kv_nw, a_nw, w_k, w_vt, w_q, cos, sin_lo, sin_hi)


def _diff_attn_kernel(lam_ref, q_ref, qn_ref, k_ref, kn_ref, vt_ref, swt_ref, o_ref,
                      sx_scr, s0_scr, s1_scr, cx_scr, c0_scr, c1_scr, m_scr, acc_scr, *, lam_init, n_steps, bq):
    bk = bq
    hd = LANES // 2
    G = q_ref.shape[0] // bq
    j = pl.program_id(2)
    s_bufs = (s0_scr, s1_scr, sx_scr)
    c_bufs = (c0_scr, c1_scr, cx_scr)
    X = 2

    def stacked(q):
        lane = lax.broadcasted_iota(jnp.int32, q.shape, 1)
        zero = jnp.zeros_like(q)
        return jnp.concatenate([jnp.where(lane < hd, q, zero), jnp.where(lane >= hd, q, zero)], axis=0)

    def score(k_rows, q_stacked, buf):
        s = _dot_nt(k_rows, q_stacked)
        s_bufs[buf][...] = s
        c_bufs[buf][...] = jnp.max(s, axis=0, keepdims=True)

    def softmax_pv(n, buf, masked):
        s = s_bufs[buf][...]
        m_prev = m_scr[...]
        if masked:
            kk = lax.broadcasted_iota(jnp.int32, (bk, bq), 0)
            qq = lax.broadcasted_iota(jnp.int32, (bk, bq), 1)
            keep = kk <= qq
            s = jnp.where(jnp.concatenate([keep, keep], axis=1), s, -jnp.inf)
            m_new = jnp.maximum(m_prev, jnp.max(s, axis=0, keepdims=True))
        else:
            m_new = jnp.maximum(m_prev, c_bufs[buf][...])
        alpha = jnp.exp2(m_prev - m_new)
        p = jnp.exp2(s - m_new).astype(BF16)
        acc_scr[...] = alpha * acc_scr[...] + _dot(vt_ref[:, pl.ds(n * bk, bk)], p)
        m_scr[...] = m_new

    @pl.when((pl.program_id(0) == 0) & (pl.program_id(1) == 0) & (j == 0))
    def _():
        score(k_ref[pl.ds(0, bk), :], stacked(q_ref[pl.ds(0, bq), :]), X)

    def query_block(ii, g):
        qs = stacked(q_ref[pl.ds(g * bq, bq), :])
        m_scr[...] = jnp.full_like(m_scr, -jnp.inf)
        acc_scr[...] = jnp.zeros_like(acc_scr)
        if ii > 0:
            score(k_ref[pl.ds(bk, bk), :], qs, 1)
            softmax_pv(0, X, False)
            for n in range(1, ii):
                score(k_ref[pl.ds((n + 1) * bk, bk), :], qs, (n + 1) % 2)
                softmax_pv(n, n % 2, False)
        softmax_pv(ii, ii % 2 if ii > 0 else X, True)
        if g + 1 < G:
            score(k_ref[pl.ds(0, bk), :], stacked(q_ref[pl.ds((g + 1) * bq, bq), :]), X)
        else:
            score(kn_ref[...], stacked(qn_ref[...]), X)
        lp = lam_ref[...]
        lam = (jnp.exp(jnp.sum(lp[0:1, :] * lp[1:2, :], axis=-1, keepdims=True))
               - jnp.exp(jnp.sum(lp[2:3, :] * lp[3:4, :], axis=-1, keepdims=True)) + lam_init)
        acc = acc_scr[...]
        o_all = acc[:LANES, :] / acc[LANES:LANES + 1, :]
        ot = o_all[:, :bq] - lam * o_all[:, bq:]
        ms = jnp.mean(ot * ot, axis=0, keepdims=True)
        ot = ot * lax.rsqrt(ms + EPS) * swt_ref[...] * (1.0 - lam_init)
        o_ref[pl.ds(g * bq, bq), :] = ot.T.astype(BF16)

    def whole_step(jj):
        def run():
            for g in range(G):
                query_block(jj * G + g, g)
        return run

    lax.switch(j, [whole_step(jj) for jj in range(n_steps)])


def _diff_attn(q4, k4, vt4, lam_p, subln_col, lam_init):
    B, H, S, _ = q4.shape
    bq = ATTN_BLOCK
    G = ATTN_GROUP
    n_steps = S // (bq * G)

    def next_step(b, h, j):
        j2 = j + 1
        h2 = h + j2 // n_steps
        b2 = b + h2 // H
        last = b2 >= B
        return (jnp.where(last, b, b2), jnp.where(last, h, h2 % H), jnp.where(last, j, j2 % n_steps))

    def q_next(b, h, j):
        b2, h2, j2 = next_step(b, h, j)
        return (b2, h2, j2 * G, 0)

    def k_next(b, h, j):
        b2, h2, _ = next_step(b, h, j)
        return (b2, h2, 0, 0)

    return pl.pallas_call(
        functools.partial(_diff_attn_kernel, lam_init=lam_init, n_steps=n_steps, bq=bq),
        grid=(B, H, n_steps),
        in_specs=[
            _full(lam_p.shape),
            pl.BlockSpec((None, None, G * bq, LANES), lambda b, h, j: (b, h, j, 0)),
            pl.BlockSpec((None, None, bq, LANES), q_next),
            pl.BlockSpec((None, None, S, LANES), lambda b, h, j: (b, h, 0, 0)),
            pl.BlockSpec((None, None, bq, LANES), k_next),
            pl.BlockSpec((None, None, LANES + SUBLANES, S), lambda b, h, j: (b, h, 0, 0)),
            _full(subln_col.shape),
        ],
        out_specs=pl.BlockSpec((None, None, G * bq, LANES), lambda b, h, j: (b, h, j, 0)),
        out_shape=jax.ShapeDtypeStruct((B, H, S, LANES), BF16),
        scratch_shapes=[
            pltpu.VMEM((bq, 2 * bq), F32),
            pltpu.VMEM((bq, 2 * bq), F32),
            pltpu.VMEM((bq, 2 * bq), F32),
            pltpu.VMEM((1, 2 * bq), F32),
            pltpu.VMEM((1, 2 * bq), F32),
            pltpu.VMEM((1, 2 * bq), F32),
            pltpu.VMEM((1, 2 * bq), F32),
            pltpu.VMEM((LANES + SUBLANES, 2 * bq), F32),
        ],
        compiler_params=_params(3),
        name="diff_attn",
    )(lam_p, q4, q4, k4, k4, vt4, subln_col)


def _rope_tables(seq, hd):
    pos = jnp.arange(seq, dtype=F32)
    inv = ROPE_THETA ** (-jnp.arange(0, hd, 2, dtype=F32) / hd)
    f = pos[:, None] * inv[None, :]
    cos_h, sin_h = jnp.cos(f), jnp.sin(f)
    zeros = jnp.zeros_like(sin_h)
    cos = jnp.concatenate([cos_h, cos_h] * 2, axis=-1)
    sin_lo = jnp.concatenate([-sin_h, zeros] * 2, axis=-1)
    sin_hi = jnp.concatenate([zeros, sin_h] * 2, axis=-1)
    return cos, sin_lo, sin_hi


def kernel(x, attn_norm_w, ffn_norm_w, gla_w_qkvg, gla_w_gk1, gla_w_gk2, gla_b_gk, gla_onorm_w, gla_w_o, kv_norm_w, w_kv, diff_w_q, diff_lambda, diff_subln_w, diff_w_o, ffn_w_in, ffn_conv_w, ffn_conv_b, ffn_w_out, final_norm_w):
    B, S, D = x.shape
    depth = attn_norm_w.shape[0]
    n_a = gla_w_qkvg.shape[0]
    dk = gla_w_gk2.shape[2]
    dv = gla_w_o.shape[1]
    n_qk = diff_w_q.shape[2]
    hd = n_qk // (2 * DIFF_HEADS)
    assert hd * 2 == LANES and (w_kv.shape[1] - n_qk) == DIFF_HEADS * LANES
    assert S % TOK_BLOCK == 0 and S % (ATTN_BLOCK * ATTN_GROUP) == 0 and TOK_BLOCK % GLA_CHUNK == 0
    assert ffn_w_out.shape[1] % FFN_CHUNK == 0 and GLA_GATE_RANK <= LANES

    row = lambda v: v.reshape(1, -1)
    cos, sin_lo, sin_hi = _rope_tables(S, hd)

    ffn_stacks = (ffn_norm_w[:, None, :], ffn_w_in.astype(BF16), ffn_conv_w, ffn_conv_b[:, None, :],
                  ffn_w_out.astype(BF16))
    h = x
    for l in range(depth):
        final_nw = row(final_norm_w) if l == depth - 1 else None
        ffn_args = (l,) + ffn_stacks
        if l < n_a:
            w1_pad = jnp.pad(gla_w_gk1[l], ((0, 0), (0, LANES - GLA_GATE_RANK)))
            w_cat = jnp.concatenate([gla_w_qkvg[l], w1_pad], axis=1).astype(BF16)
            w2_pad = jnp.pad(gla_w_gk2[l], ((0, LANES - GLA_GATE_RANK), (0, 0))).astype(BF16)
            qkvg, gk = _gla_proj(h.reshape(B * S, D), row(attn_norm_w[l]), w_cat, w2_pad,
                                 row(gla_b_gk[l]))
            h = _gla_core(qkvg.reshape(B, S, -1), gk.reshape(B, S, dk), h, row(gla_onorm_w[l]),
                          gla_w_o[l].astype(BF16), dk, dv)
            h = _ffn(h, *ffn_args, final_nw=final_nw)
        else:
            j = l - n_a
            lam_init = 0.8 - 0.6 * math.exp(-0.3 * l)
            q3, k_new, vt_new = _qkv_proj(h, row(kv_norm_w), row(attn_norm_w[l]),
                                          w_kv[:, :n_qk].astype(BF16), w_kv[:, n_qk:].T.astype(BF16),
                                          diff_w_q[j].astype(BF16), cos, sin_lo, sin_hi,
                                          float(hd) ** -0.5 * LOG2E)
            if l == n_a:
                k3, vt4 = k_new, vt_new
            o3 = _diff_attn(q3, k3, vt4, diff_lambda[j], diff_subln_w[j].reshape(-1, 1), lam_init)
            h = _ffn(h, *ffn_args, mix3=o3, wo=diff_w_o[j].astype(BF16), final_nw=final_nw)
    return h
```

```python
import functools
import math

import jax
import jax.numpy as jnp
from jax import lax
from jax.experimental import pallas as pl
from jax.experimental.pallas import tpu as pltpu

F32 = jnp.float32
BF16 = jnp.bfloat16

EPS = 1e-6
ROPE_THETA = 10000.0
GLA_HEADS = 4
GLA_CHUNK = 64
GLA_GATE_RANK = 16
GLA_GATE_NORM = 16.0
DIFF_HEADS = 8
CONV_WIDTH = 3

LANES = 128
SUBLANES = 8
VMEM_LIMIT = 56 * 1024 * 1024

TOK_BLOCK = 512
FFN_CHUNK = 256
ATTN_BLOCK = 512
ATTN_GROUP = 4
LOG2E = math.log2(math.e)


def _params(n_axes):
    return pltpu.CompilerParams(
        dimension_semantics=("arbitrary",) * n_axes, vmem_limit_bytes=VMEM_LIMIT)


def _full(shape):
    nd = len(shape)
    return pl.BlockSpec(shape, lambda *_: (0,) * nd)


def _layer_slab(shape, layer):
    return pl.BlockSpec((None,) + tuple(shape[1:]), lambda *_: (layer, 0, 0), pipeline_mode=pl.Buffered(1))


def _rms(x, w):
    ms = jnp.mean(x * x, axis=-1, keepdims=True)
    return x * lax.rsqrt(ms + EPS) * w


def _sigmoid(x):
    return 1.0 / (1.0 + jnp.exp2(x * (-LOG2E)))


def _dot(a, b):
    return jnp.dot(a, b, preferred_element_type=F32)


def _dot_nt(a, b):
    return lax.dot_general(a, b, (((1,), (1,)), ((), ())), preferred_element_type=F32)


def _dot_tn(a, b):
    return lax.dot_general(a, b, (((0,), (0,)), ((), ())), preferred_element_type=F32)


def _gla_proj_kernel(x_ref, nw_ref, w_ref, w2_ref, b_ref, qkvg_ref, gk_ref, *, n_main):
    hn = _rms(x_ref[...], nw_ref[...]).astype(BF16)
    proj = _dot(hn, w_ref[...])
    qkvg_ref[...] = proj[:, :n_main].astype(BF16)
    low = proj[:, n_main:].astype(BF16)
    z = _dot(low, w2_ref[...]) + b_ref[...]
    gk_ref[...] = (jnp.minimum(z, 0.0) - jnp.log(1.0 + jnp.exp(-jnp.abs(z)))) * (1.0 / GLA_GATE_NORM)


def _gla_proj(x2, nw, w_cat, w2_pad, b_gk):
    T, D = x2.shape
    n_main = w_cat.shape[1] - LANES
    dk = w2_pad.shape[1]
    tm = TOK_BLOCK
    return pl.pallas_call(
        functools.partial(_gla_proj_kernel, n_main=n_main),
        grid=(T // tm,),
        in_specs=[
            pl.BlockSpec((tm, D), lambda i: (i, 0)),
            _full(nw.shape), _full(w_cat.shape), _full(w2_pad.shape), _full(b_gk.shape),
        ],
        out_specs=[
            pl.BlockSpec((tm, n_main), lambda i: (i, 0)),
            pl.BlockSpec((tm, dk), lambda i: (i, 0)),
        ],
        out_shape=[
            jax.ShapeDtypeStruct((T, n_main), BF16),
            jax.ShapeDtypeStruct((T, dk), F32),
        ],
        compiler_params=_params(1),
        name="gla_proj",
    )(x2, nw, w_cat, w2_pad, b_gk)


def _cumsum_rows(g):
    n = g.shape[0]
    ridx = lax.broadcasted_iota(jnp.int32, g.shape, 0)
    b = g
    s = 1
    while s < n:
        b = b + jnp.where(ridx >= s, pltpu.roll(b, s, axis=0), 0.0)
        s *= 2
    return b


def _gla_core_kernel(qkvg_ref, gk_ref, x_ref, onw_ref, wo_ref, out_ref, state_ref, o_scr,
                     *, dk, dv):
    C = GLA_CHUNK
    H = GLA_HEADS
    hk = dk // H
    hv = dv // H
    tb = x_ref.shape[0]

    @pl.when(pl.program_id(1) == 0)
    def _():
        state_ref[...] = jnp.zeros_like(state_ref)

    row = lax.broadcasted_iota(jnp.int32, (C, C), 0)
    col = lax.broadcasted_iota(jnp.int32, (C, C), 1)
    causal = col <= row
    qscale = hk ** -0.5

    for c in range(tb // C):
        r = pl.ds(c * C, C)
        b_all = _cumsum_rows(gk_ref[r, :])
        for h in range(H):
            q = qkvg_ref[r, h * hk:(h + 1) * hk].astype(F32)
            k = qkvg_ref[r, dk + h * hk:dk + (h + 1) * hk].astype(F32)
            v = qkvg_ref[r, 2 * dk + h * hv:2 * dk + (h + 1) * hv]
            b = b_all[:, h * hk:(h + 1) * hk]
            b_last = b[C - 1:C, :]
            q_in = (q * qscale * jnp.exp(b)).astype(BF16)
            k_in = (k * jnp.exp(-b)).astype(BF16)
            k_end = (k * jnp.exp(b_last - b)).astype(BF16)
            scores = jnp.where(causal, _dot_nt(q_in, k_in), 0.0)
            st = state_ref[h]
            o = _dot(scores.astype(BF16), v) + _dot_nt(q_in, st.astype(BF16))
            o_scr[r, h * hv:(h + 1) * hv] = o
            state_ref[h] = st * jnp.exp(b_last) + _dot_tn(v, k_end)

    onw = onw_ref[...]
    parts = []
    for h in range(H):
        parts.append(_rms(o_scr[:, h * hv:(h + 1) * hv], onw))
    on = jnp.concatenate(parts, axis=1)
    g = qkvg_ref[:, 2 * dk + dv:2 * dk + 2 * dv].astype(F32)
    gated = (on * (g * _sigmoid(g))).astype(BF16)
    out_ref[...] = x_ref[...] + _dot(gated, wo_ref[...])


def _gla_core(qkvg3, gk3, x3, onw, wo, dk, dv):
    B, S, D = x3.shape
    tb = TOK_BLOCK
    H = GLA_HEADS
    return pl.pallas_call(
        functools.partial(_gla_core_kernel, dk=dk, dv=dv),
        grid=(B, S // tb),
        in_specs=[
            pl.BlockSpec((None, tb, qkvg3.shape[2]), lambda b, s: (b, s, 0)),
            pl.BlockSpec((None, tb, dk), lambda b, s: (b, s, 0)),
            pl.BlockSpec((None, tb, D), lambda b, s: (b, s, 0)),
            _full(onw.shape), _full(wo.shape),
        ],
        out_specs=pl.BlockSpec((None, tb, D), lambda b, s: (b, s, 0)),
        out_shape=jax.ShapeDtypeStruct((B, S, D), F32),
        scratch_shapes=[
            pltpu.VMEM((H, dv // H, dk // H), F32),
            pltpu.VMEM((tb, dv), F32),
        ],
        compiler_params=_params(2),
        name="gla_core",
    )(qkvg3, gk3, x3, onw, wo)


def _ffn_kernel(*refs, has_mix, final_norm, d_ff):
    refs = list(refs)
    h_ref = refs.pop(0)
    if has_mix:
        mix_ref = refs.pop(0)
        wo_ref = refs.pop(0)
    nw_ref, win_ref, cw_ref, cb_ref, wout_ref = refs[:5]
    refs = refs[5:]
    if final_norm:
        fnw_ref = refs.pop(0)
    out_ref, perm_ref, u_ref, carry_ref, act_ref = refs

    tm, d_model = h_ref.shape
    fc = FFN_CHUNK
    sub = SUBLANES
    seg = tm // sub
    pitch = seg + sub
    n_slabs = d_model // LANES

    @pl.when(pl.program_id(1) == 0)
    def _():
        carry_ref[...] = jnp.zeros_like(carry_ref)

    h = h_ref[...]
    if has_mix:
        mix = jnp.concatenate([mix_ref[i] for i in range(mix_ref.shape[0])], axis=1)
        h = h + _dot(mix, wo_ref[...])

    def interleave(x):
        for c in range(n_slabs):
            for s_ in range(sub):
                perm_ref[c, pl.ds(s_ * pitch, seg), :] = x[s_ * seg:(s_ + 1) * seg, c * LANES:(c + 1) * LANES]
        cols = []
        for c in range(n_slabs):
            cols.append(jnp.concatenate(
                [perm_ref[c, pl.ds(j, sub, stride=pitch), :] for j in range(seg)], axis=0))
        return jnp.concatenate(cols, axis=1)

    def deinterleave_store(y):
        for c in range(n_slabs):
            for j in range(seg):
                perm_ref[c, pl.ds(j, sub, stride=pitch), :] = y[j * sub:(j + 1) * sub, c * LANES:(c + 1) * LANES]
        for c in range(n_slabs):
            for s_ in range(sub):
                out_ref[s_ * seg:(s_ + 1) * seg, c * LANES:(c + 1) * LANES] = perm_ref[c, pl.ds(s_ * pitch, seg), :]

    h = interleave(h)
    hn = _rms(h, nw_ref[...]).astype(BF16)
    first_row = lax.broadcasted_iota(jnp.int32, (sub, fc), 0) == 0

    def up(j):
        for part in range(2):
            u_ref[2 * j + part] = _dot(hn, win_ref[:, pl.ds(part * d_ff + j * fc, fc)])

    def shift(y, e, k):
        top = pltpu.roll(y[tm - sub:, :], 1, axis=0)
        fixed = jnp.where(first_row, carry_ref[e, k], top)
        carry_ref[e, k] = top
        return jnp.concatenate([fixed, y[:tm - sub, :]], axis=0)

    def conv(j, part):
        e = 2 * j + part
        cols = pl.ds(part * d_ff + j * fc, fc)
        u = u_ref[e]
        cw = cw_ref[:, cols]
        y = shift(u * cw[0:1, :], e, 0) + u * cw[1:2, :]
        return shift(y, e, 1) + u * cw[2:3, :] + cb_ref[:, cols]

    n_chunks = d_ff // fc
    up(0)
    for j in range(n_chunks):
        if j + 1 < n_chunks:
            up(j + 1)
        a = conv(j, 0)
        g = conv(j, 1)
        act_ref[:, pl.ds(j * fc, fc)] = (g * _sigmoid(g) * a).astype(BF16)
    out = h + _dot(act_ref[...], wout_ref[...])
    if final_norm:
        out = _rms(out, fnw_ref[...])
    deinterleave_store(out)


def _ffn(h3, layer, nw, w_in, conv_w, conv_b, w_out, mix3=None, wo=None, final_nw=None):
    B, S, D = h3.shape
    tm = TOK_BLOCK
    d_ff = w_out.shape[1]
    has_mix = mix3 is not None
    final_norm = final_nw is not None
    tok = pl.BlockSpec((None, tm, D), lambda b, s: (b, s, 0))
    args, specs = [h3], [tok]
    if has_mix:
        args += [mix3, wo]
        specs += [pl.BlockSpec((None, mix3.shape[1], tm, mix3.shape[3]), lambda b, s: (b, 0, s, 0)),
                  _full(wo.shape)]
    args += [nw, w_in, conv_w, conv_b, w_out]
    specs += [_layer_slab(a.shape, layer) for a in (nw, w_in, conv_w, conv_b, w_out)]
    if final_norm:
        args.append(final_nw)
        specs.append(_full(final_nw.shape))
    n_chunks = d_ff // FFN_CHUNK
    return pl.pallas_call(
        functools.partial(_ffn_kernel, has_mix=has_mix, final_norm=final_norm, d_ff=d_ff),
        grid=(B, S // tm),
        in_specs=specs,
        out_specs=tok,
        out_shape=jax.ShapeDtypeStruct((B, S, D), F32),
        scratch_shapes=[pltpu.VMEM((D // LANES, tm + SUBLANES * SUBLANES, LANES), F32),
                        pltpu.VMEM((2 * n_chunks, tm, FFN_CHUNK), F32),
                        pltpu.VMEM((2 * n_chunks, 2, SUBLANES, FFN_CHUNK), F32),
                        pltpu.VMEM((tm, d_ff), BF16)],
        compiler_params=_params(2),
        name="ffn_mix" if has_mix else "ffn",
    )(*args)


def _rope(x, cos, sin_lo, sin_hi):
    half = LANES // 4
    return (x * cos + pltpu.roll(x, LANES - half, axis=1) * sin_lo
            + pltpu.roll(x, half, axis=1) * sin_hi)


def _qkv_proj_kernel(h_ref, kvnw_ref, anw_ref, wk_ref, wvt_ref, wq_ref, cos_ref, slo_ref, shi_ref,
                     q_ref, k_ref, vt_ref, *, q_scale):
    x = h_ref[...]
    ms = jnp.mean(x * x, axis=-1, keepdims=True)
    xn = x * lax.rsqrt(ms + EPS)
    kvn = (xn * kvnw_ref[...]).astype(BF16)
    an = (xn * anw_ref[...]).astype(BF16)
    k = _dot(kvn, wk_ref[...])
    q = _dot(an, wq_ref[...])
    vt = _dot_nt(wvt_ref[...], kvn)
    cos = cos_ref[...]
    slo = slo_ref[...]
    shi = shi_ref[...]
    for j in range(q.shape[1] // LANES):
        cols = slice(j * LANES, (j + 1) * LANES)
        q_ref[j] = (_rope(q[:, cols], cos, slo, shi) * q_scale).astype(BF16)
        k_ref[j] = _rope(k[:, cols], cos, slo, shi).astype(BF16)
    ones = jnp.ones((SUBLANES, vt.shape[1]), BF16)
    for h in range(vt_ref.shape[0]):
        vt_ref[h, :LANES, :] = vt[h * LANES:(h + 1) * LANES, :].astype(BF16)
        vt_ref[h, LANES:, :] = ones


def _qkv_proj(h3, kv_nw, a_nw, w_k, w_vt, w_q, cos, sin_lo, sin_hi, q_scale):
    B, S, D = h3.shape
    tm = TOK_BLOCK
    n_qk = w_q.shape[1]
    H = w_vt.shape[0] // LANES
    tok = lambda n: pl.BlockSpec((None, tm, n), lambda b, s: (b, s, 0))
    tab = pl.BlockSpec((tm, LANES), lambda b, s: (s, 0))
    return pl.pallas_call(
        functools.partial(_qkv_proj_kernel, q_scale=q_scale),
        grid=(B, S // tm),
        in_specs=[tok(D), _full(kv_nw.shape), _full(a_nw.shape), _full(w_k.shape), _full(w_vt.shape),
                  _full(w_q.shape), tab, tab, tab],
        out_specs=[pl.BlockSpec((None, H, tm, LANES), lambda b, s: (b, 0, s, 0)),
                   pl.BlockSpec((None, H, tm, LANES), lambda b, s: (b, 0, s, 0)),
                   pl.BlockSpec((None, H, LANES + SUBLANES, tm), lambda b, s: (b, 0, 0, s))],
        out_shape=[jax.ShapeDtypeStruct((B, H, S, LANES), BF16),
                   jax.ShapeDtypeStruct((B, H, S, LANES), BF16),
                   jax.ShapeDtypeStruct((B, H, LANES + SUBLANES, S), BF16)],
        compiler_params=_params(2),
        name="qkv_proj",
    )(h3, kv_nw, a_nw, w_k, w_vt, w_q, cos, sin_lo, sin_hi)


def _diff_attn_kernel(lam_ref, q_ref, qn_ref, k_ref, kn_ref, vt_ref, swt_ref, o_ref,
                      sx_scr, s0_scr, s1_scr, cx_scr, c0_scr, c1_scr, m_scr, acc_scr, *, lam_init, n_steps, bq):
    bk = bq
    hd = LANES // 2
    G = q_ref.shape[0] // bq
    j = pl.program_id(2)
    s_bufs = (s0_scr, s1_scr, sx_scr)
    c_bufs = (c0_scr, c1_scr, cx_scr)
    X = 2

    def stacked(q):
        lane = lax.broadcasted_iota(jnp.int32, q.shape, 1)
        zero = jnp.zeros_like(q)
        return jnp.concatenate([jnp.where(lane < hd, q, zero), jnp.where(lane >= hd, q, zero)], axis=0)

    def score(k_rows, q_stacked, buf):
        s = _dot_nt(k_rows, q_stacked)
        s_bufs[buf][...] = s
        c_bufs[buf][...] = jnp.max(s, axis=0, keepdims=True)

    def softmax_pv(n, buf, masked):
        s = s_bufs[buf][...]
        m_prev = m_scr[...]
        if masked:
            kk = lax.broadcasted_iota(jnp.int32, (bk, bq), 0)
            qq = lax.broadcasted_iota(jnp.int32, (bk, bq), 1)
            keep = kk <= qq
            s = jnp.where(jnp.concatenate([keep, keep], axis=1), s, -jnp.inf)
            m_new = jnp.maximum(m_prev, jnp.max(s, axis=0, keepdims=True))
        else:
            m_new = jnp.maximum(m_prev, c_bufs[buf][...])
        alpha = jnp.exp2(m_prev - m_new)
        p = jnp.exp2(s - m_new).astype(BF16)
        acc_scr[...] = alpha * acc_scr[...] + _dot(vt_ref[:, pl.ds(n * bk, bk)], p)
        m_scr[...] = m_new

    @pl.when((pl.program_id(0) == 0) & (pl.program_id(1) == 0) & (j == 0))
    def _():
        score(k_ref[pl.ds(0, bk), :], stacked(q_ref[pl.ds(0, bq), :]), X)

    def query_block(ii, g):
        qs = stacked(q_ref[pl.ds(g * bq, bq), :])
        m_scr[...] = jnp.full_like(m_scr, -jnp.inf)
        acc_scr[...] = jnp.zeros_like(acc_scr)
        if ii > 0:
            score(k_ref[pl.ds(bk, bk), :], qs, 1)
            softmax_pv(0, X, False)
            for n in range(1, ii):
                score(k_ref[pl.ds((n + 1) * bk, bk), :], qs, (n + 1) % 2)
                softmax_pv(n, n % 2, False)
        softmax_pv(ii, ii % 2 if ii > 0 else X, True)
        if g + 1 < G:
            score(k_ref[pl.ds(0, bk), :], stacked(q_ref[pl.ds((g + 1) * bq, bq), :]), X)
        else:
            score(kn_ref[...], stacked(qn_ref[...]), X)
        lp = lam_ref[...]
        lam = (jnp.exp(jnp.sum(lp[0:1, :] * lp[1:2, :], axis=-1, keepdims=True))
               - jnp.exp(jnp.sum(lp[2:3, :] * lp[3:4, :], axis=-1, keepdims=True)) + lam_init)
        acc = acc_scr[...]
        o_all = acc[:LANES, :] / acc[LANES:LANES + 1, :]
        ot = o_all[:, :bq] - lam * o_all[:, bq:]
        ms = jnp.mean(ot * ot, axis=0, keepdims=True)
        ot = ot * lax.rsqrt(ms + EPS) * swt_ref[...] * (1.0 - lam_init)
        o_ref[pl.ds(g * bq, bq), :] = ot.T.astype(BF16)

    def whole_step(jj):
        def run():
            for g in range(G):
                query_block(jj * G + g, g)
        return run

    lax.switch(j, [whole_step(jj) for jj in range(n_steps)])


def _diff_attn(q4, k4, vt4, lam_p, subln_col, lam_init):
    B, H, S, _ = q4.shape
    bq = ATTN_BLOCK
    G = ATTN_GROUP
    n_steps = S // (bq * G)

    def next_step(b, h, j):
        j2 = j + 1
        h2 = h + j2 // n_steps
        b2 = b + h2 // H
        last = b2 >= B
        return (jnp.where(last, b, b2), jnp.where(last, h, h2 % H), jnp.where(last, j, j2 % n_steps))

    def q_next(b, h, j):
        b2, h2, j2 = next_step(b, h, j)
        return (b2, h2, j2 * G, 0)

    def k_next(b, h, j):
        b2, h2, _ = next_step(b, h, j)
        return (b2, h2, 0, 0)

    return pl.pallas_call(
        functools.partial(_diff_attn_kernel, lam_init=lam_init, n_steps=n_steps, bq=bq),
        grid=(B, H, n_steps),
        in_specs=[
            _full(lam_p.shape),
            pl.BlockSpec((None, None, G * bq, LANES), lambda b, h, j: (b, h, j, 0)),
            pl.BlockSpec((None, None, bq, LANES), q_next),
            pl.BlockSpec((None, None, S, LANES), lambda b, h, j: (b, h, 0, 0)),
            pl.BlockSpec((None, None, bq, LANES), k_next),
            pl.BlockSpec((None, None, LANES + SUBLANES, S), lambda b, h, j: (b, h, 0, 0)),
            _full(subln_col.shape),
        ],
        out_specs=pl.BlockSpec((None, None, G * bq, LANES), lambda b, h, j: (b, h, j, 0)),
        out_shape=jax.ShapeDtypeStruct((B, H, S, LANES), BF16),
        scratch_shapes=[
            pltpu.VMEM((bq, 2 * bq), F32),
            pltpu.VMEM((bq, 2 * bq), F32),
            pltpu.VMEM((bq, 2 * bq), F32),
            pltpu.VMEM((1, 2 * bq), F32),
            pltpu.VMEM((1, 2 * bq), F32),
            pltpu.VMEM((1, 2 * bq), F32),
            pltpu.VMEM((1, 2 * bq), F32),
            pltpu.VMEM((LANES + SUBLANES, 2 * bq), F32),
        ],
        compiler_params=_params(3),
        name="diff_attn",
    )(lam_p, q4, q4, k4, k4, vt4, subln_col)


def _rope_tables(seq, hd):
    pos = jnp.arange(seq, dtype=F32)
    inv = ROPE_THETA ** (-jnp.arange(0, hd, 2, dtype=F32) / hd)
    f = pos[:, None] * inv[None, :]
    cos_h, sin_h = jnp.cos(f), jnp.sin(f)
    zeros = jnp.zeros_like(sin_h)
    cos = jnp.concatenate([cos_h, cos_h] * 2, axis=-1)
    sin_lo = jnp.concatenate([-sin_h, zeros] * 2, axis=-1)
    sin_hi = jnp.concatenate([zeros, sin_h] * 2, axis=-1)
    return cos, sin_lo, sin_hi


def kernel(x, attn_norm_w, ffn_norm_w, gla_w_qkvg, gla_w_gk1, gla_w_gk2, gla_b_gk, gla_onorm_w, gla_w_o, kv_norm_w, w_kv, diff_w_q, diff_lambda, diff_subln_w, diff_w_o, ffn_w_in, ffn_conv_w, ffn_conv_b, ffn_w_out, final_norm_w):
    B, S, D = x.shape
    depth = attn_norm_w.shape[0]
    n_a = gla_w_qkvg.shape[0]
    dk = gla_w_gk2.shape[2]
    dv = gla_w_o.shape[1]
    n_qk = diff_w_q.shape[2]
    hd = n_qk // (2 * DIFF_HEADS)
    assert hd * 2 == LANES and (w_kv.shape[1] - n_qk) == DIFF_HEADS * LANES
    assert S % TOK_BLOCK == 0 and S % (ATTN_BLOCK * ATTN_GROUP) == 0 and TOK_BLOCK % GLA_CHUNK == 0
    assert ffn_w_out.shape[1] % FFN_CHUNK == 0 and GLA_GATE_RANK <= LANES
    assert ffn_conv_w.shape[1] == CONV_WIDTH

    row = lambda v: v.reshape(1, -1)
    cos, sin_lo, sin_hi = _rope_tables(S, hd)

    ffn_stacks = (ffn_norm_w[:, None, :], ffn_w_in.astype(BF16), ffn_conv_w, ffn_conv_b[:, None, :],
                  ffn_w_out.astype(BF16))
    h = x
    for l in range(depth):
        final_nw = row(final_norm_w) if l == depth - 1 else None
        ffn_args = (l,) + ffn_stacks
        if l < n_a:
            w1_pad = jnp.pad(gla_w_gk1[l], ((0, 0), (0, LANES - GLA_GATE_RANK)))
            w_cat = jnp.concatenate([gla_w_qkvg[l], w1_pad], axis=1).astype(BF16)
            w2_pad = jnp.pad(gla_w_gk2[l], ((0, LANES - GLA_GATE_RANK), (0, 0))).astype(BF16)
            qkvg, gk = _gla_proj(h.reshape(B * S, D), row(attn_norm_w[l]), w_cat, w2_pad,
                                 row(gla_b_gk[l]))
            h = _gla_core(qkvg.reshape(B, S, -1), gk.reshape(B, S, dk), h, row(gla_onorm_w[l]),
                          gla_w_o[l].astype(BF16), dk, dv)
            h = _ffn(h, *ffn_args, final_nw=final_nw)
        else:
            j = l - n_a
            lam_init = 0.8 - 0.6 * math.exp(-0.3 * l)
            q3, k_new, vt_new = _qkv_proj(h, row(kv_norm_w), row(attn_norm_w[l]),
                                          w_kv[:, :n_qk].astype(BF16), w_kv[:, n_qk:].T.astype(BF16),
                                          diff_w_q[j].astype(BF16), cos, sin_lo, sin_hi,
                                          float(hd) ** -0.5 * LOG2E)
            if l == n_a:
                k3, vt4 = k_new, vt_new
            o3 = _diff_attn(q3, k3, vt4, diff_lambda[j], diff_subln_w[j].reshape(-1, 1), lam_init)
            h = _ffn(h, *ffn_args, mix3=o3, wo=diff_w_o[j].astype(BF16), final_nw=final_nw)
    return h
```

```python
import functools
import math

import jax
import jax.numpy as jnp
from jax import lax
from jax.experimental import pallas as pl
from jax.experimental.pallas import tpu as pltpu

F32 = jnp.float32
BF16 = jnp.bfloat16

EPS = 1e-6
ROPE_THETA = 10000.0
GLA_HEADS = 4
GLA_CHUNK = 64
GLA_GATE_RANK = 16
GLA_GATE_NORM = 16.0
DIFF_HEADS = 8
CONV_WIDTH = 3

LANES = 128
SUBLANES = 8
VMEM_LIMIT = 56 * 1024 * 1024

TOK_BLOCK = 512
PROJ_BLOCK = 1024
FFN_CHUNK = 256
ATTN_BLOCK = 512
ATTN_GROUP = 4
LOG2E = math.log2(math.e)


def _params(n_axes):
    return pltpu.CompilerParams(
        dimension_semantics=("arbitrary",) * n_axes, vmem_limit_bytes=VMEM_LIMIT)


def _full(shape):
    nd = len(shape)
    return pl.BlockSpec(shape, lambda *_: (0,) * nd)


def _layer_slab(shape, layer):
    return pl.BlockSpec((None,) + tuple(shape[1:]), lambda *_: (layer, 0, 0), pipeline_mode=pl.Buffered(1))


def _rms(x, w):
    ms = jnp.mean(x * x, axis=-1, keepdims=True)
    return x * lax.rsqrt(ms + EPS) * w


def _sigmoid(x):
    return 1.0 / (1.0 + jnp.exp2(x * (-LOG2E)))


def _dot(a, b):
    return jnp.dot(a, b, preferred_element_type=F32)


def _dot_nt(a, b):
    return lax.dot_general(a, b, (((1,), (1,)), ((), ())), preferred_element_type=F32)


def _dot_tn(a, b):
    return lax.dot_general(a, b, (((0,), (0,)), ((), ())), preferred_element_type=F32)


def _gla_proj_kernel(x_ref, nw_ref, w_ref, w2_ref, b_ref, qkvg_ref, gk_ref, *, n_main):
    hn = _rms(x_ref[...], nw_ref[...]).astype(BF16)
    proj = _dot(hn, w_ref[...])
    qkvg_ref[...] = proj[:, :n_main].astype(BF16)
    low = proj[:, n_main:].astype(BF16)
    z = _dot(low, w2_ref[...]) + b_ref[...]
    gk_ref[...] = (jnp.minimum(z, 0.0) - jnp.log(1.0 + jnp.exp(-jnp.abs(z)))) * (1.0 / GLA_GATE_NORM)


def _gla_proj(x2, nw, w_cat, w2_pad, b_gk):
    T, D = x2.shape
    n_main = w_cat.shape[1] - LANES
    dk = w2_pad.shape[1]
    tm = PROJ_BLOCK
    return pl.pallas_call(
        functools.partial(_gla_proj_kernel, n_main=n_main),
        grid=(T // tm,),
        in_specs=[
            pl.BlockSpec((tm, D), lambda i: (i, 0)),
            _full(nw.shape), _full(w_cat.shape), _full(w2_pad.shape), _full(b_gk.shape),
        ],
        out_specs=[
            pl.BlockSpec((tm, n_main), lambda i: (i, 0)),
            pl.BlockSpec((tm, dk), lambda i: (i, 0)),
        ],
        out_shape=[
            jax.ShapeDtypeStruct((T, n_main), BF16),
            jax.ShapeDtypeStruct((T, dk), F32),
        ],
        compiler_params=_params(1),
        name="gla_proj",
    )(x2, nw, w_cat, w2_pad, b_gk)


def _cumsum_rows(g):
    n = g.shape[0]
    ridx = lax.broadcasted_iota(jnp.int32, g.shape, 0)
    b = g
    s = 1
    while s < n:
        b = b + jnp.where(ridx >= s, pltpu.roll(b, s, axis=0), 0.0)
        s *= 2
    return b


def _gla_core_kernel(qkvg_ref, gk_ref, x_ref, onw_ref, wo_ref, out_ref, state_ref, o_scr,
                     *, dk, dv):
    C = GLA_CHUNK
    H = GLA_HEADS
    hk = dk // H
    hv = dv // H
    tb = x_ref.shape[0]

    @pl.when(pl.program_id(1) == 0)
    def _():
        state_ref[...] = jnp.zeros_like(state_ref)

    row = lax.broadcasted_iota(jnp.int32, (C, C), 0)
    col = lax.broadcasted_iota(jnp.int32, (C, C), 1)
    causal = col <= row
    qscale = hk ** -0.5

    for c in range(tb // C):
        r = pl.ds(c * C, C)
        b_all = _cumsum_rows(gk_ref[r, :])
        for h in range(H):
            q = qkvg_ref[r, h * hk:(h + 1) * hk].astype(F32)
            k = qkvg_ref[r, dk + h * hk:dk + (h + 1) * hk].astype(F32)
            v = qkvg_ref[r, 2 * dk + h * hv:2 * dk + (h + 1) * hv]
            b = b_all[:, h * hk:(h + 1) * hk]
            b_last = b[C - 1:C, :]
            q_in = (q * qscale * jnp.exp(b)).astype(BF16)
            k_in = (k * jnp.exp(-b)).astype(BF16)
            k_end = (k * jnp.exp(b_last - b)).astype(BF16)
            scores = jnp.where(causal, _dot_nt(q_in, k_in), 0.0)
            st = state_ref[h]
            o = _dot(scores.astype(BF16), v) + _dot_nt(q_in, st.astype(BF16))
            o_scr[r, h * hv:(h + 1) * hv] = o
            state_ref[h] = st * jnp.exp(b_last) + _dot_tn(v, k_end)

    onw = onw_ref[...]
    parts = []
    for h in range(H):
        parts.append(_rms(o_scr[:, h * hv:(h + 1) * hv], onw))
    on = jnp.concatenate(parts, axis=1)
    g = qkvg_ref[:, 2 * dk + dv:2 * dk + 2 * dv].astype(F32)
    gated = (on * (g * _sigmoid(g))).astype(BF16)
    out_ref[...] = x_ref[...] + _dot(gated, wo_ref[...])


def _gla_core(qkvg3, gk3, x3, onw, wo, dk, dv):
    B, S, D = x3.shape
    tb = TOK_BLOCK
    H = GLA_HEADS
    return pl.pallas_call(
        functools.partial(_gla_core_kernel, dk=dk, dv=dv),
        grid=(B, S // tb),
        in_specs=[
            pl.BlockSpec((None, tb, qkvg3.shape[2]), lambda b, s: (b, s, 0)),
            pl.BlockSpec((None, tb, dk), lambda b, s: (b, s, 0)),
            pl.BlockSpec((None, tb, D), lambda b, s: (b, s, 0)),
            _full(onw.shape), _full(wo.shape),
        ],
        out_specs=pl.BlockSpec((None, tb, D), lambda b, s: (b, s, 0)),
        out_shape=jax.ShapeDtypeStruct((B, S, D), F32),
        scratch_shapes=[
            pltpu.VMEM((H, dv // H, dk // H), F32),
            pltpu.VMEM((tb, dv), F32),
        ],
        compiler_params=_params(2),
        name="gla_core",
    )(qkvg3, gk3, x3, onw, wo)


def _ffn_kernel(*refs, has_mix, final_norm, d_ff):
    refs = list(refs)
    h_ref = refs.pop(0)
    if has_mix:
        mix_ref = refs.pop(0)
        wo_ref = refs.pop(0)
    nw_ref, win_ref, cw_ref, cb_ref, wout_ref = refs[:5]
    refs = refs[5:]
    if final_norm:
        fnw_ref = refs.pop(0)
    out_ref, perm_ref, u_ref, carry_ref, act_ref = refs

    tm, d_model = h_ref.shape
    fc = FFN_CHUNK
    sub = SUBLANES
    seg = tm // sub
    pitch = seg + sub
    n_slabs = d_model // LANES

    @pl.when(pl.program_id(1) == 0)
    def _():
        carry_ref[...] = jnp.zeros_like(carry_ref)

    h = h_ref[...]
    if has_mix:
        mix = jnp.concatenate([mix_ref[i] for i in range(mix_ref.shape[0])], axis=1)
        h = h + _dot(mix, wo_ref[...])

    def interleave(x):
        for c in range(n_slabs):
            for s_ in range(sub):
                perm_ref[c, pl.ds(s_ * pitch, seg), :] = x[s_ * seg:(s_ + 1) * seg, c * LANES:(c + 1) * LANES]
        cols = []
        for c in range(n_slabs):
            cols.append(jnp.concatenate(
                [perm_ref[c, pl.ds(j, sub, stride=pitch), :] for j in range(seg)], axis=0))
        return jnp.concatenate(cols, axis=1)

    def deinterleave_store(y):
        for c in range(n_slabs):
            for j in range(seg):
                perm_ref[c, pl.ds(j, sub, stride=pitch), :] = y[j * sub:(j + 1) * sub, c * LANES:(c + 1) * LANES]
        for c in range(n_slabs):
            for s_ in range(sub):
                out_ref[s_ * seg:(s_ + 1) * seg, c * LANES:(c + 1) * LANES] = perm_ref[c, pl.ds(s_ * pitch, seg), :]

    h = interleave(h)
    hn = _rms(h, nw_ref[...]).astype(BF16)
    first_row = lax.broadcasted_iota(jnp.int32, (sub, fc), 0) == 0

    def up(j):
        for part in range(2):
            u_ref[2 * j + part] = _dot(hn, win_ref[:, pl.ds(part * d_ff + j * fc, fc)])

    def shift(y, e, k):
        top = pltpu.roll(y[tm - sub:, :], 1, axis=0)
        fixed = jnp.where(first_row, carry_ref[e, k], top)
        carry_ref[e, k] = top
        return jnp.concatenate([fixed, y[:tm - sub, :]], axis=0)

    def conv(j, part):
        e = 2 * j + part
        cols = pl.ds(part * d_ff + j * fc, fc)
        u = u_ref[e]
        cw = cw_ref[:, cols]
        y = shift(u * cw[0:1, :], e, 0) + u * cw[1:2, :]
        return shift(y, e, 1) + u * cw[2:3, :] + cb_ref[:, cols]

    n_chunks = d_ff // fc
    up(0)
    for j in range(n_chunks):
        if j + 1 < n_chunks:
            up(j + 1)
        a = conv(j, 0)
        g = conv(j, 1)
        act_ref[:, pl.ds(j * fc, fc)] = (g * _sigmoid(g) * a).astype(BF16)
    out = h + _dot(act_ref[...], wout_ref[...])
    if final_norm:
        out = _rms(out, fnw_ref[...])
    deinterleave_store(out)


def _ffn(h3, layer, nw, w_in, conv_w, conv_b, w_out, mix3=None, wo=None, final_nw=None):
    B, S, D = h3.shape
    tm = TOK_BLOCK
    d_ff = w_out.shape[1]
    has_mix = mix3 is not None
    final_norm = final_nw is not None
    tok = pl.BlockSpec((None, tm, D), lambda b, s: (b, s, 0))
    args, specs = [h3], [tok]
    if has_mix:
        args += [mix3, wo]
        specs += [pl.BlockSpec((None, mix3.shape[1], tm, mix3.shape[3]), lambda b, s: (b, 0, s, 0)),
                  _full(wo.shape)]
    args += [nw, w_in, conv_w, conv_b, w_out]
    specs += [_layer_slab(a.shape, layer) for a in (nw, w_in, conv_w, conv_b, w_out)]
    if final_norm:
        args.append(final_nw)
        specs.append(_full(final_nw.shape))
    n_chunks = d_ff // FFN_CHUNK
    return pl.pallas_call(
        functools.partial(_ffn_kernel, has_mix=has_mix, final_norm=final_norm, d_ff=d_ff),
        grid=(B, S // tm),
        in_specs=specs,
        out_specs=tok,
        out_shape=jax.ShapeDtypeStruct((B, S, D), F32),
        scratch_shapes=[pltpu.VMEM((D // LANES, tm + SUBLANES * SUBLANES, LANES), F32),
                        pltpu.VMEM((2 * n_chunks, tm, FFN_CHUNK), F32),
                        pltpu.VMEM((2 * n_chunks, 2, SUBLANES, FFN_CHUNK), F32),
                        pltpu.VMEM((tm, d_ff), BF16)],
        compiler_params=_params(2),
        name="ffn_mix" if has_mix else "ffn",
    )(*args)


def _rope(x, cos, sin_lo, sin_hi):
    half = LANES // 4
    return (x * cos + pltpu.roll(x, LANES - half, axis=1) * sin_lo
            + pltpu.roll(x, half, axis=1) * sin_hi)


def _qkv_proj_kernel(h_ref, kvnw_ref, anw_ref, wk_ref, wvt_ref, wq_ref, cos_ref, slo_ref, shi_ref,
                     q_ref, k_ref, vt_ref, *, q_scale):
    x = h_ref[...]
    ms = jnp.mean(x * x, axis=-1, keepdims=True)
    xn = x * lax.rsqrt(ms + EPS)
    kvn = (xn * kvnw_ref[...]).astype(BF16)
    an = (xn * anw_ref[...]).astype(BF16)
    k = _dot(kvn, wk_ref[...])
    q = _dot(an, wq_ref[...])
    vt = _dot_nt(wvt_ref[...], kvn)
    cos = cos_ref[...]
    slo = slo_ref[...]
    shi = shi_ref[...]
    for j in range(q.shape[1] // LANES):
        cols = slice(j * LANES, (j + 1) * LANES)
        q_ref[j] = (_rope(q[:, cols], cos, slo, shi) * q_scale).astype(BF16)
        k_ref[j] = _rope(k[:, cols], cos, slo, shi).astype(BF16)
    ones = jnp.ones((SUBLANES, vt.shape[1]), BF16)
    for h in range(vt_ref.shape[0]):
        vt_ref[h, :LANES, :] = vt[h * LANES:(h + 1) * LANES, :].astype(BF16)
        vt_ref[h, LANES:, :] = ones


def _qkv_proj(h3, kv_nw, a_nw, w_k, w_vt, w_q, cos, sin_lo, sin_hi, q_scale):
    B, S, D = h3.shape
    tm = PROJ_BLOCK
    n_qk = w_q.shape[1]
    H = w_vt.shape[0] // LANES
    tok = lambda n: pl.BlockSpec((None, tm, n), lambda b, s: (b, s, 0))
    tab = pl.BlockSpec((tm, LANES), lambda b, s: (s, 0))
    return pl.pallas_call(
        functools.partial(_qkv_proj_kernel, q_scale=q_scale),
        grid=(B, S // tm),
        in_specs=[tok(D), _full(kv_nw.shape), _full(a_nw.shape), _full(w_k.shape), _full(w_vt.shape),
                  _full(w_q.shape), tab, tab, tab],
        out_specs=[pl.BlockSpec((None, H, tm, LANES), lambda b, s: (b, 0, s, 0)),
                   pl.BlockSpec((None, H, tm, LANES), lambda b, s: (b, 0, s, 0)),
                   pl.BlockSpec((None, H, LANES + SUBLANES, tm), lambda b, s: (b, 0, 0, s))],
        out_shape=[jax.ShapeDtypeStruct((B, H, S, LANES), BF16),
                   jax.ShapeDtypeStruct((B, H, S, LANES), BF16),
                   jax.ShapeDtypeStruct((B, H, LANES + SUBLANES, S), BF16)],
        compiler_params=_params(2),
        name="qkv_proj",
    )(h3, kv_nw, a_nw, w_k, w_vt, w_q, cos, sin_lo, sin_hi)


def _diff_attn_kernel(lam_ref, q_ref, qn_ref, k_ref, kn_ref, vt_ref, swt_ref, o_ref,
                      sx_scr, s0_scr, s1_scr, cx_scr, c0_scr, c1_scr, m_scr, acc_scr, *, lam_init, n_steps, bq):
    bk = bq
    hd = LANES // 2
    G = q_ref.shape[0] // bq
    j = pl.program_id(2)
    s_bufs = (s0_scr, s1_scr, sx_scr)
    c_bufs = (c0_scr, c1_scr, cx_scr)
    X = 2

    def stacked(q):
        lane = lax.broadcasted_iota(jnp.int32, q.shape, 1)
        zero = jnp.zeros_like(q)
        return jnp.concatenate([jnp.where(lane < hd, q, zero), jnp.where(lane >= hd, q, zero)], axis=0)

    def score(k_rows, q_stacked, buf):
        s = _dot_nt(k_rows, q_stacked)
        s_bufs[buf][...] = s
        c_bufs[buf][...] = jnp.max(s, axis=0, keepdims=True)

    def softmax_pv(n, buf, masked):
        s = s_bufs[buf][...]
        m_prev = m_scr[...]
        if masked:
            kk = lax.broadcasted_iota(jnp.int32, (bk, bq), 0)
            qq = lax.broadcasted_iota(jnp.int32, (bk, bq), 1)
            keep = kk <= qq
            s = jnp.where(jnp.concatenate([keep, keep], axis=1), s, -jnp.inf)
            m_new = jnp.maximum(m_prev, jnp.max(s, axis=0, keepdims=True))
        else:
            m_new = jnp.maximum(m_prev, c_bufs[buf][...])
        alpha = jnp.exp2(m_prev - m_new)
        p = jnp.exp2(s - m_new).astype(BF16)
        acc_scr[...] = alpha * acc_scr[...] + _dot(vt_ref[:, pl.ds(n * bk, bk)], p)
        m_scr[...] = m_new

    @pl.when((pl.program_id(0) == 0) & (pl.program_id(1) == 0) & (j == 0))
    def _():
        score(k_ref[pl.ds(0, bk), :], stacked(q_ref[pl.ds(0, bq), :]), X)

    def query_block(ii, g):
        qs = stacked(q_ref[pl.ds(g * bq, bq), :])
        m_scr[...] = jnp.full_like(m_scr, -jnp.inf)
        acc_scr[...] = jnp.zeros_like(acc_scr)
        if ii > 0:
            score(k_ref[pl.ds(bk, bk), :], qs, 1)
            softmax_pv(0, X, False)
            for n in range(1, ii):
                score(k_ref[pl.ds((n + 1) * bk, bk), :], qs, (n + 1) % 2)
                softmax_pv(n, n % 2, False)
        softmax_pv(ii, ii % 2 if ii > 0 else X, True)
        if g + 1 < G:
            score(k_ref[pl.ds(0, bk), :], stacked(q_ref[pl.ds((g + 1) * bq, bq), :]), X)
        else:
            score(kn_ref[...], stacked(qn_ref[...]), X)
        lp = lam_ref[...]
        lam = (jnp.exp(jnp.sum(lp[0:1, :] * lp[1:2, :], axis=-1, keepdims=True))
               - jnp.exp(jnp.sum(lp[2:3, :] * lp[3:4, :], axis=-1, keepdims=True)) + lam_init)
        acc = acc_scr[...]
        o_all = acc[:LANES, :] / acc[LANES:LANES + 1, :]
        ot = o_all[:, :bq] - lam * o_all[:, bq:]
        ms = jnp.mean(ot * ot, axis=0, keepdims=True)
        ot = ot * lax.rsqrt(ms + EPS) * swt_ref[...] * (1.0 - lam_init)
        o_ref[pl.ds(g * bq, bq), :] = ot.T.astype(BF16)

    def whole_step(jj):
        def run():
            for g in range(G):
                query_block(jj * G + g, g)
        return run

    lax.switch(j, [whole_step(jj) for jj in range(n_steps)])


def _diff_attn(q4, k4, vt4, lam_p, subln_col, lam_init):
    B, H, S, _ = q4.shape
    bq = ATTN_BLOCK
    G = ATTN_GROUP
    n_steps = S // (bq * G)

    def next_step(b, h, j):
        j2 = j + 1
        h2 = h + j2 // n_steps
        b2 = b + h2 // H
        last = b2 >= B
        return (jnp.where(last, b, b2), jnp.where(last, h, h2 % H), jnp.where(last, j, j2 % n_steps))

    def q_next(b, h, j):
        b2, h2, j2 = next_step(b, h, j)
        return (b2, h2, j2 * G, 0)

    def k_next(b, h, j):
        b2, h2, _ = next_step(b, h, j)
        return (b2, h2, 0, 0)

    return pl.pallas_call(
        functools.partial(_diff_attn_kernel, lam_init=lam_init, n_steps=n_steps, bq=bq),
        grid=(B, H, n_steps),
        in_specs=[
            _full(lam_p.shape),
            pl.BlockSpec((None, None, G * bq, LANES), lambda b, h, j: (b, h, j, 0)),
            pl.BlockSpec((None, None, bq, LANES), q_next),
            pl.BlockSpec((None, None, S, LANES), lambda b, h, j: (b, h, 0, 0)),
            pl.BlockSpec((None, None, bq, LANES), k_next),
            pl.BlockSpec((None, None, LANES + SUBLANES, S), lambda b, h, j: (b, h, 0, 0)),
            _full(subln_col.shape),
        ],
        out_specs=pl.BlockSpec((None, None, G * bq, LANES), lambda b, h, j: (b, h, j, 0)),
        out_shape=jax.ShapeDtypeStruct((B, H, S, LANES), BF16),
        scratch_shapes=[
            pltpu.VMEM((bq, 2 * bq), F32),
            pltpu.VMEM((bq, 2 * bq), F32),
            pltpu.VMEM((bq, 2 * bq), F32),
            pltpu.VMEM((1, 2 * bq), F32),
            pltpu.VMEM((1, 2 * bq), F32),
            pltpu.VMEM((1, 2 * bq), F32),
            pltpu.VMEM((1, 2 * bq), F32),
            pltpu.VMEM((LANES + SUBLANES, 2 * bq), F32),
        ],
        compiler_params=_params(3),
        name="diff_attn",
    )(lam_p, q4, q4, k4, k4, vt4, subln_col)


def _rope_tables(seq, hd):
    pos = jnp.arange(seq, dtype=F32)
    inv = ROPE_THETA ** (-jnp.arange(0, hd, 2, dtype=F32) / hd)
    f = pos[:, None] * inv[None, :]
    cos_h, sin_h = jnp.cos(f), jnp.sin(f)
    zeros = jnp.zeros_like(sin_h)
    cos = jnp.concatenate([cos_h, cos_h] * 2, axis=-1)
    sin_lo = jnp.concatenate([-sin_h, zeros] * 2, axis=-1)
    sin_hi = jnp.concatenate([zeros, sin_h] * 2, axis=-1)
    return cos, sin_lo, sin_hi


def kernel(x, attn_norm_w, ffn_norm_w, gla_w_qkvg, gla_w_gk1, gla_w_gk2, gla_b_gk, gla_onorm_w, gla_w_o, kv_norm_w, w_kv, diff_w_q, diff_lambda, diff_subln_w, diff_w_o, ffn_w_in, ffn_conv_w, ffn_conv_b, ffn_w_out, final_norm_w):
    B, S, D = x.shape
    depth = attn_norm_w.shape[0]
    n_a = gla_w_qkvg.shape[0]
    dk = gla_w_gk2.shape[2]
    dv = gla_w_o.shape[1]
    n_qk = diff_w_q.shape[2]
    hd = n_qk // (2 * DIFF_HEADS)
    assert hd * 2 == LANES and (w_kv.shape[1] - n_qk) == DIFF_HEADS * LANES
    assert S % TOK_BLOCK == 0 and S % (ATTN_BLOCK * ATTN_GROUP) == 0 and TOK_BLOCK % GLA_CHUNK == 0
    assert (B * S) % PROJ_BLOCK == 0 and S % PROJ_BLOCK == 0
    assert ffn_w_out.shape[1] % FFN_CHUNK == 0 and GLA_GATE_RANK <= LANES
    assert ffn_conv_w.shape[1] == CONV_WIDTH

    row = lambda v: v.reshape(1, -1)
    cos, sin_lo, sin_hi = _rope_tables(S, hd)

    ffn_stacks = (ffn_norm_w[:, None, :], ffn_w_in.astype(BF16), ffn_conv_w, ffn_conv_b[:, None, :],
                  ffn_w_out.astype(BF16))
    h = x
    for l in range(depth):
        final_nw = row(final_norm_w) if l == depth - 1 else None
        ffn_args = (l,) + ffn_stacks
        if l < n_a:
            w1_pad = jnp.pad(gla_w_gk1[l], ((0, 0), (0, LANES - GLA_GATE_RANK)))
            w_cat = jnp.concatenate([gla_w_qkvg[l], w1_pad], axis=1).astype(BF16)
            w2_pad = jnp.pad(gla_w_gk2[l], ((0, LANES - GLA_GATE_RANK), (0, 0))).astype(BF16)
            qkvg, gk = _gla_proj(h.reshape(B * S, D), row(attn_norm_w[l]), w_cat, w2_pad,
                                 row(gla_b_gk[l]))
            h = _gla_core(qkvg.reshape(B, S, -1), gk.reshape(B, S, dk), h, row(gla_onorm_w[l]),
                          gla_w_o[l].astype(BF16), dk, dv)
            h = _ffn(h, *ffn_args, final_nw=final_nw)
        else:
            j = l - n_a
            lam_init = 0.8 - 0.6 * math.exp(-0.3 * l)
            q3, k_new, vt_new = _qkv_proj(h, row(kv_norm_w), row(attn_norm_w[l]),
                                          w_kv[:, :n_qk].astype(BF16), w_kv[:, n_qk:].T.astype(BF16),
                                          diff_w_q[j].astype(BF16), cos, sin_lo, sin_hi,
                                          float(hd) ** -0.5 * LOG2E)
            if l == n_a:
                k3, vt4 = k_new, vt_new
            o3 = _diff_attn(q3, k3, vt4, diff_lambda[j], diff_subln_w[j].reshape(-1, 1), lam_init)
            h = _ffn(h, *ffn_args, mix3=o3, wo=diff_w_o[j].astype(BF16), final_nw=final_nw)
    return h
```

```python
import functools
import math

import jax
import jax.numpy as jnp
from jax import lax
from jax.experimental import pallas as pl
from jax.experimental.pallas import tpu as pltpu

F32 = jnp.float32
BF16 = jnp.bfloat16

EPS = 1e-6
ROPE_THETA = 10000.0
GLA_HEADS = 4
GLA_CHUNK = 64
GLA_GATE_RANK = 16
GLA_GATE_NORM = 16.0
DIFF_HEADS = 8
CONV_WIDTH = 3

LANES = 128
SUBLANES = 8
VMEM_LIMIT = 56 * 1024 * 1024

TOK_BLOCK = 512
PROJ_BLOCK = 1024
FFN_CHUNK = 256
ATTN_BLOCK = 512
ATTN_GROUP = 4
LOG2E = math.log2(math.e)


def _params(n_axes):
    return pltpu.CompilerParams(
        dimension_semantics=("arbitrary",) * n_axes, vmem_limit_bytes=VMEM_LIMIT)


def _full(shape):
    nd = len(shape)
    return pl.BlockSpec(shape, lambda *_: (0,) * nd)


def _layer_slab(shape, layer):
    return pl.BlockSpec((None,) + tuple(shape[1:]), lambda *_: (layer, 0, 0), pipeline_mode=pl.Buffered(1))


def _rms(x, w):
    ms = jnp.mean(x * x, axis=-1, keepdims=True)
    return x * lax.rsqrt(ms + EPS) * w


def _sigmoid(x):
    return 1.0 / (1.0 + jnp.exp2(x * (-LOG2E)))


def _dot(a, b):
    return jnp.dot(a, b, preferred_element_type=F32)


def _dot_nt(a, b):
    return lax.dot_general(a, b, (((1,), (1,)), ((), ())), preferred_element_type=F32)


def _dot_tn(a, b):
    return lax.dot_general(a, b, (((0,), (0,)), ((), ())), preferred_element_type=F32)


def _gla_proj_kernel(x_ref, nw_ref, w_ref, w2_ref, b_ref, qkvg_ref, gk_ref, *, n_main):
    hn = _rms(x_ref[...], nw_ref[...]).astype(BF16)
    proj = _dot(hn, w_ref[...])
    qkvg_ref[...] = proj[:, :n_main].astype(BF16)
    low = proj[:, n_main:].astype(BF16)
    z = _dot(low, w2_ref[...]) + b_ref[...]
    gk_ref[...] = (jnp.minimum(z, 0.0) - jnp.log(1.0 + jnp.exp(-jnp.abs(z)))) * (1.0 / GLA_GATE_NORM)


def _gla_proj(x2, nw, w_cat, w2_pad, b_gk):
    T, D = x2.shape
    n_main = w_cat.shape[1] - LANES
    dk = w2_pad.shape[1]
    tm = PROJ_BLOCK
    return pl.pallas_call(
        functools.partial(_gla_proj_kernel, n_main=n_main),
        grid=(T // tm,),
        in_specs=[
            pl.BlockSpec((tm, D), lambda i: (i, 0)),
            _full(nw.shape), _full(w_cat.shape), _full(w2_pad.shape), _full(b_gk.shape),
        ],
        out_specs=[
            pl.BlockSpec((tm, n_main), lambda i: (i, 0)),
            pl.BlockSpec((tm, dk), lambda i: (i, 0)),
        ],
        out_shape=[
            jax.ShapeDtypeStruct((T, n_main), BF16),
            jax.ShapeDtypeStruct((T, dk), F32),
        ],
        compiler_params=_params(1),
        name="gla_proj",
    )(x2, nw, w_cat, w2_pad, b_gk)


def _gla_core_kernel(qkvg_ref, gk_ref, x_ref, onw_ref, wo_ref, out_ref, state_ref, o_scr,
                     *, dk, dv):
    C = GLA_CHUNK
    H = GLA_HEADS
    hk = dk // H
    hv = dv // H
    tb = x_ref.shape[0]

    @pl.when(pl.program_id(1) == 0)
    def _():
        state_ref[...] = jnp.zeros_like(state_ref)

    n_chunks = tb // C
    row = lax.broadcasted_iota(jnp.int32, (tb, tb), 0)
    col = lax.broadcasted_iota(jnp.int32, (tb, tb), 1)
    intra = (col <= row) & (col >= (row // C) * C)
    qscale = hk ** -0.5

    ridx = lax.broadcasted_iota(jnp.int32, (tb, dk), 0) % C
    b_all = gk_ref[...]
    sft = 1
    while sft < C:
        b_all = b_all + jnp.where(ridx >= sft, pltpu.roll(b_all, sft, axis=0), 0.0)
        sft *= 2
    b_end = jnp.concatenate(
        [jnp.broadcast_to(b_all[(c + 1) * C - 1:(c + 1) * C, :], (C, dk)) for c in range(n_chunks)], axis=0)

    for h in range(H):
        hs = slice(h * hk, (h + 1) * hk)
        q = qkvg_ref[:, h * hk:(h + 1) * hk].astype(F32)
        k = qkvg_ref[:, dk + h * hk:dk + (h + 1) * hk].astype(F32)
        v = qkvg_ref[:, 2 * dk + h * hv:2 * dk + (h + 1) * hv]
        b = b_all[:, hs]
        q_in = (q * qscale * jnp.exp(b)).astype(BF16)
        k_in = (k * jnp.exp(-b)).astype(BF16)
        k_end = (k * jnp.exp(b_end[:, hs] - b)).astype(BF16)
        scores = jnp.where(intra, _dot_nt(q_in, k_in), 0.0)
        o_intra = _dot(scores.astype(BF16), v)
        for c in range(n_chunks):
            r = slice(c * C, (c + 1) * C)
            st = state_ref[h]
            o_scr[r, h * hv:(h + 1) * hv] = o_intra[r, :] + _dot_nt(q_in[r, :], st.astype(BF16))
            state_ref[h] = st * jnp.exp(b_end[(c + 1) * C - 1:(c + 1) * C, hs]) + _dot_tn(v[r, :], k_end[r, :])

    onw = onw_ref[...]
    parts = []
    for h in range(H):
        parts.append(_rms(o_scr[:, h * hv:(h + 1) * hv], onw))
    on = jnp.concatenate(parts, axis=1)
    g = qkvg_ref[:, 2 * dk + dv:2 * dk + 2 * dv].astype(F32)
    gated = (on * (g * _sigmoid(g))).astype(BF16)
    out_ref[...] = x_ref[...] + _dot(gated, wo_ref[...])


def _gla_core(qkvg3, gk3, x3, onw, wo, dk, dv):
    B, S, D = x3.shape
    tb = TOK_BLOCK
    H = GLA_HEADS
    return pl.pallas_call(
        functools.partial(_gla_core_kernel, dk=dk, dv=dv),
        grid=(B, S // tb),
        in_specs=[
            pl.BlockSpec((None, tb, qkvg3.shape[2]), lambda b, s: (b, s, 0)),
            pl.BlockSpec((None, tb, dk), lambda b, s: (b, s, 0)),
            pl.BlockSpec((None, tb, D), lambda b, s: (b, s, 0)),
            _full(onw.shape), _full(wo.shape),
        ],
        out_specs=pl.BlockSpec((None, tb, D), lambda b, s: (b, s, 0)),
        out_shape=jax.ShapeDtypeStruct((B, S, D), F32),
        scratch_shapes=[
            pltpu.VMEM((H, dv // H, dk // H), F32),
            pltpu.VMEM((tb, dv), F32),
        ],
        compiler_params=_params(2),
        name="gla_core",
    )(qkvg3, gk3, x3, onw, wo)


def _ffn_kernel(*refs, has_mix, final_norm, d_ff):
    refs = list(refs)
    h_ref = refs.pop(0)
    if has_mix:
        mix_ref = refs.pop(0)
        wo_ref = refs.pop(0)
    nw_ref, win_ref, cw_ref, cb_ref, wout_ref = refs[:5]
    refs = refs[5:]
    if final_norm:
        fnw_ref = refs.pop(0)
    out_ref, perm_ref, u_ref, carry_ref, act_ref = refs

    tm, d_model = h_ref.shape
    fc = FFN_CHUNK
    sub = SUBLANES
    seg = tm // sub
    pitch = seg + sub
    n_slabs = d_model // LANES

    @pl.when(pl.program_id(1) == 0)
    def _():
        carry_ref[...] = jnp.zeros_like(carry_ref)

    h = h_ref[...]
    if has_mix:
        mix = jnp.concatenate([mix_ref[i] for i in range(mix_ref.shape[0])], axis=1)
        h = h + _dot(mix, wo_ref[...])

    def interleave(x):
        for c in range(n_slabs):
            for s_ in range(sub):
                perm_ref[c, pl.ds(s_ * pitch, seg), :] = x[s_ * seg:(s_ + 1) * seg, c * LANES:(c + 1) * LANES]
        cols = []
        for c in range(n_slabs):
            cols.append(jnp.concatenate(
                [perm_ref[c, pl.ds(j, sub, stride=pitch), :] for j in range(seg)], axis=0))
        return jnp.concatenate(cols, axis=1)

    def deinterleave_store(y):
        for c in range(n_slabs):
            for j in range(seg):
                perm_ref[c, pl.ds(j, sub, stride=pitch), :] = y[j * sub:(j + 1) * sub, c * LANES:(c + 1) * LANES]
        for c in range(n_slabs):
            for s_ in range(sub):
                out_ref[s_ * seg:(s_ + 1) * seg, c * LANES:(c + 1) * LANES] = perm_ref[c, pl.ds(s_ * pitch, seg), :]

    h = interleave(h)
    hn = _rms(h, nw_ref[...]).astype(BF16)
    first_row = lax.broadcasted_iota(jnp.int32, (sub, fc), 0) == 0

    def up(j):
        for part in range(2):
            u_ref[2 * j + part] = _dot(hn, win_ref[:, pl.ds(part * d_ff + j * fc, fc)])

    def shift(y, e, k):
        top = pltpu.roll(y[tm - sub:, :], 1, axis=0)
        fixed = jnp.where(first_row, carry_ref[e, k], top)
        carry_ref[e, k] = top
        return jnp.concatenate([fixed, y[:tm - sub, :]], axis=0)

    def conv(j, part):
        e = 2 * j + part
        cols = pl.ds(part * d_ff + j * fc, fc)
        u = u_ref[e]
        cw = cw_ref[:, cols]
        y = shift(u * cw[0:1, :], e, 0) + u * cw[1:2, :]
        return shift(y, e, 1) + u * cw[2:3, :] + cb_ref[:, cols]

    n_chunks = d_ff // fc
    up(0)
    for j in range(n_chunks):
        if j + 1 < n_chunks:
            up(j + 1)
        a = conv(j, 0)
        g = conv(j, 1)
        act_ref[:, pl.ds(j * fc, fc)] = (g * _sigmoid(g) * a).astype(BF16)
    out = h + _dot(act_ref[...], wout_ref[...])
    if final_norm:
        out = _rms(out, fnw_ref[...])
    deinterleave_store(out)


def _ffn(h3, layer, nw, w_in, conv_w, conv_b, w_out, mix3=None, wo=None, final_nw=None):
    B, S, D = h3.shape
    tm = TOK_BLOCK
    d_ff = w_out.shape[1]
    has_mix = mix3 is not None
    final_norm = final_nw is not None
    tok = pl.BlockSpec((None, tm, D), lambda b, s: (b, s, 0))
    args, specs = [h3], [tok]
    if has_mix:
        args += [mix3, wo]
        specs += [pl.BlockSpec((None, mix3.shape[1], tm, mix3.shape[3]), lambda b, s: (b, 0, s, 0)),
                  _full(wo.shape)]
    args += [nw, w_in, conv_w, conv_b, w_out]
    specs += [_layer_slab(a.shape, layer) for a in (nw, w_in, conv_w, conv_b, w_out)]
    if final_norm:
        args.append(final_nw)
        specs.append(_full(final_nw.shape))
    n_chunks = d_ff // FFN_CHUNK
    return pl.pallas_call(
        functools.partial(_ffn_kernel, has_mix=has_mix, final_norm=final_norm, d_ff=d_ff),
        grid=(B, S // tm),
        in_specs=specs,
        out_specs=tok,
        out_shape=jax.ShapeDtypeStruct((B, S, D), F32),
        scratch_shapes=[pltpu.VMEM((D // LANES, tm + SUBLANES * SUBLANES, LANES), F32),
                        pltpu.VMEM((2 * n_chunks, tm, FFN_CHUNK), F32),
                        pltpu.VMEM((2 * n_chunks, 2, SUBLANES, FFN_CHUNK), F32),
                        pltpu.VMEM((tm, d_ff), BF16)],
        compiler_params=_params(2),
        name="ffn_mix" if has_mix else "ffn",
    )(*args)


def _rope(x, cos, sin_lo, sin_hi):
    half = LANES // 4
    return (x * cos + pltpu.roll(x, LANES - half, axis=1) * sin_lo
            + pltpu.roll(x, half, axis=1) * sin_hi)


def _qkv_proj_kernel(h_ref, kvnw_ref, anw_ref, wk_ref, wvt_ref, wq_ref, cos_ref, slo_ref, shi_ref,
                     q_ref, k_ref, vt_ref, *, q_scale):
    x = h_ref[...]
    ms = jnp.mean(x * x, axis=-1, keepdims=True)
    xn = x * lax.rsqrt(ms + EPS)
    kvn = (xn * kvnw_ref[...]).astype(BF16)
    an = (xn * anw_ref[...]).astype(BF16)
    k = _dot(kvn, wk_ref[...])
    q = _dot(an, wq_ref[...])
    vt = _dot_nt(wvt_ref[...], kvn)
    cos = cos_ref[...]
    slo = slo_ref[...]
    shi = shi_ref[...]
    for j in range(q.shape[1] // LANES):
        cols = slice(j * LANES, (j + 1) * LANES)
        q_ref[j] = (_rope(q[:, cols], cos, slo, shi) * q_scale).astype(BF16)
        k_ref[j] = _rope(k[:, cols], cos, slo, shi).astype(BF16)
    ones = jnp.ones((SUBLANES, vt.shape[1]), BF16)
    for h in range(vt_ref.shape[0]):
        vt_ref[h, :LANES, :] = vt[h * LANES:(h + 1) * LANES, :].astype(BF16)
        vt_ref[h, LANES:, :] = ones


def _qkv_proj(h3, kv_nw, a_nw, w_k, w_vt, w_q, cos, sin_lo, sin_hi, q_scale):
    B, S, D = h3.shape
    tm = PROJ_BLOCK
    n_qk = w_q.shape[1]
    H = w_vt.shape[0] // LANES
    tok = lambda n: pl.BlockSpec((None, tm, n), lambda b, s: (b, s, 0))
    tab = pl.BlockSpec((tm, LANES), lambda b, s: (s, 0))
    return pl.pallas_call(
        functools.partial(_qkv_proj_kernel, q_scale=q_scale),
        grid=(B, S // tm),
        in_specs=[tok(D), _full(kv_nw.shape), _full(a_nw.shape), _full(w_k.shape), _full(w_vt.shape),
                  _full(w_q.shape), tab, tab, tab],
        out_specs=[pl.BlockSpec((None, H, tm, LANES), lambda b, s: (b, 0, s, 0)),
                   pl.BlockSpec((None, H, tm, LANES), lambda b, s: (b, 0, s, 0)),
                   pl.BlockSpec((None, H, LANES + SUBLANES, tm), lambda b, s: (b, 0, 0, s))],
        out_shape=[jax.ShapeDtypeStruct((B, H, S, LANES), BF16),
                   jax.ShapeDtypeStruct((B, H, S, LANES), BF16),
                   jax.ShapeDtypeStruct((B, H, LANES + SUBLANES, S), BF16)],
        compiler_params=_params(2),
        name="qkv_proj",
    )(h3, kv_nw, a_nw, w_k, w_vt, w_q, cos, sin_lo, sin_hi)


def _diff_attn_kernel(lam_ref, q_ref, qn_ref, k_ref, kn_ref, vt_ref, swt_ref, o_ref,
                      sx_scr, s0_scr, s1_scr, cx_scr, c0_scr, c1_scr, m_scr, acc_scr, *, lam_init, n_steps, bq):
    bk = bq
    hd = LANES // 2
    G = q_ref.shape[0] // bq
    j = pl.program_id(2)
    s_bufs = (s0_scr, s1_scr, sx_scr)
    c_bufs = (c0_scr, c1_scr, cx_scr)
    X = 2

    def stacked(q):
        lane = lax.broadcasted_iota(jnp.int32, q.shape, 1)
        zero = jnp.zeros_like(q)
        return jnp.concatenate([jnp.where(lane < hd, q, zero), jnp.where(lane >= hd, q, zero)], axis=0)

    def score(k_rows, q_stacked, buf):
        s = _dot_nt(k_rows, q_stacked)
        s_bufs[buf][...] = s
        c_bufs[buf][...] = jnp.max(s, axis=0, keepdims=True)

    def softmax_pv(n, buf, masked):
        s = s_bufs[buf][...]
        m_prev = m_scr[...]
        if masked:
            kk = lax.broadcasted_iota(jnp.int32, (bk, bq), 0)
            qq = lax.broadcasted_iota(jnp.int32, (bk, bq), 1)
            keep = kk <= qq
            s = jnp.where(jnp.concatenate([keep, keep], axis=1), s, -jnp.inf)
            m_new = jnp.maximum(m_prev, jnp.max(s, axis=0, keepdims=True))
        else:
            m_new = jnp.maximum(m_prev, c_bufs[buf][...])
        alpha = jnp.exp2(m_prev - m_new)
        p = jnp.exp2(s - m_new).astype(BF16)
        acc_scr[...] = alpha * acc_scr[...] + _dot(vt_ref[:, pl.ds(n * bk, bk)], p)
        m_scr[...] = m_new

    @pl.when((pl.program_id(0) == 0) & (pl.program_id(1) == 0) & (j == 0))
    def _():
        score(k_ref[pl.ds(0, bk), :], stacked(q_ref[pl.ds(0, bq), :]), X)

    def query_block(ii, g):
        qs = stacked(q_ref[pl.ds(g * bq, bq), :])
        m_scr[...] = jnp.full_like(m_scr, -jnp.inf)
        acc_scr[...] = jnp.zeros_like(acc_scr)
        if ii > 0:
            score(k_ref[pl.ds(bk, bk), :], qs, 1)
            softmax_pv(0, X, False)
            for n in range(1, ii):
                score(k_ref[pl.ds((n + 1) * bk, bk), :], qs, (n + 1) % 2)
                softmax_pv(n, n % 2, False)
        softmax_pv(ii, ii % 2 if ii > 0 else X, True)
        if g + 1 < G:
            score(k_ref[pl.ds(0, bk), :], stacked(q_ref[pl.ds((g + 1) * bq, bq), :]), X)
        else:
            score(kn_ref[...], stacked(qn_ref[...]), X)
        lp = lam_ref[...]
        lam = (jnp.exp(jnp.sum(lp[0:1, :] * lp[1:2, :], axis=-1, keepdims=True))
               - jnp.exp(jnp.sum(lp[2:3, :] * lp[3:4, :], axis=-1, keepdims=True)) + lam_init)
        acc = acc_scr[...]
        o_all = acc[:LANES, :] / acc[LANES:LANES + 1, :]
        ot = o_all[:, :bq] - lam * o_all[:, bq:]
        ms = jnp.mean(ot * ot, axis=0, keepdims=True)
        ot = ot * lax.rsqrt(ms + EPS) * swt_ref[...] * (1.0 - lam_init)
        o_ref[pl.ds(g * bq, bq), :] = ot.T.astype(BF16)

    def whole_step(jj):
        def run():
            for g in range(G):
                query_block(jj * G + g, g)
        return run

    lax.switch(j, [whole_step(jj) for jj in range(n_steps)])


def _diff_attn(q4, k4, vt4, lam_p, subln_col, lam_init):
    B, H, S, _ = q4.shape
    bq = ATTN_BLOCK
    G = ATTN_GROUP
    n_steps = S // (bq * G)

    def next_step(b, h, j):
        j2 = j + 1
        h2 = h + j2 // n_steps
        b2 = b + h2 // H
        last = b2 >= B
        return (jnp.where(last, b, b2), jnp.where(last, h, h2 % H), jnp.where(last, j, j2 % n_steps))

    def q_next(b, h, j):
        b2, h2, j2 = next_step(b, h, j)
        return (b2, h2, j2 * G, 0)

    def k_next(b, h, j):
        b2, h2, _ = next_step(b, h, j)
        return (b2, h2, 0, 0)

    return pl.pallas_call(
        functools.partial(_diff_attn_kernel, lam_init=lam_init, n_steps=n_steps, bq=bq),
        grid=(B, H, n_steps),
        in_specs=[
            _full(lam_p.shape),
            pl.BlockSpec((None, None, G * bq, LANES), lambda b, h, j: (b, h, j, 0)),
            pl.BlockSpec((None, None, bq, LANES), q_next),
            pl.BlockSpec((None, None, S, LANES), lambda b, h, j: (b, h, 0, 0)),
            pl.BlockSpec((None, None, bq, LANES), k_next),
            pl.BlockSpec((None, None, LANES + SUBLANES, S), lambda b, h, j: (b, h, 0, 0)),
            _full(subln_col.shape),
        ],
        out_specs=pl.BlockSpec((None, None, G * bq, LANES), lambda b, h, j: (b, h, j, 0)),
        out_shape=jax.ShapeDtypeStruct((B, H, S, LANES), BF16),
        scratch_shapes=[
            pltpu.VMEM((bq, 2 * bq), F32),
            pltpu.VMEM((bq, 2 * bq), F32),
            pltpu.VMEM((bq, 2 * bq), F32),
            pltpu.VMEM((1, 2 * bq), F32),
            pltpu.VMEM((1, 2 * bq), F32),
            pltpu.VMEM((1, 2 * bq), F32),
            pltpu.VMEM((1, 2 * bq), F32),
            pltpu.VMEM((LANES + SUBLANES, 2 * bq), F32),
        ],
        compiler_params=_params(3),
        name="diff_attn",
    )(lam_p, q4, q4, k4, k4, vt4, subln_col)


def _rope_tables(seq, hd):
    pos = jnp.arange(seq, dtype=F32)
    inv = ROPE_THETA ** (-jnp.arange(0, hd, 2, dtype=F32) / hd)
    f = pos[:, None] * inv[None, :]
    cos_h, sin_h = jnp.cos(f), jnp.sin(f)
    zeros = jnp.zeros_like(sin_h)
    cos = jnp.concatenate([cos_h, cos_h] * 2, axis=-1)
    sin_lo = jnp.concatenate([-sin_h, zeros] * 2, axis=-1)
    sin_hi = jnp.concatenate([zeros, sin_h] * 2, axis=-1)
    return cos, sin_lo, sin_hi


def kernel(x, attn_norm_w, ffn_norm_w, gla_w_qkvg, gla_w_gk1, gla_w_gk2, gla_b_gk, gla_onorm_w, gla_w_o, kv_norm_w, w_kv, diff_w_q, diff_lambda, diff_subln_w, diff_w_o, ffn_w_in, ffn_conv_w, ffn_conv_b, ffn_w_out, final_norm_w):
    B, S, D = x.shape
    depth = attn_norm_w.shape[0]
    n_a = gla_w_qkvg.shape[0]
    dk = gla_w_gk2.shape[2]
    dv = gla_w_o.shape[1]
    n_qk = diff_w_q.shape[2]
    hd = n_qk // (2 * DIFF_HEADS)
    assert hd * 2 == LANES and (w_kv.shape[1] - n_qk) == DIFF_HEADS * LANES
    assert S % TOK_BLOCK == 0 and S % (ATTN_BLOCK * ATTN_GROUP) == 0 and TOK_BLOCK % GLA_CHUNK == 0
    assert (B * S) % PROJ_BLOCK == 0 and S % PROJ_BLOCK == 0
    assert ffn_w_out.shape[1] % FFN_CHUNK == 0 and GLA_GATE_RANK <= LANES
    assert ffn_conv_w.shape[1] == CONV_WIDTH

    row = lambda v: v.reshape(1, -1)
    cos, sin_lo, sin_hi = _rope_tables(S, hd)

    ffn_stacks = (ffn_norm_w[:, None, :], ffn_w_in.astype(BF16), ffn_conv_w, ffn_conv_b[:, None, :],
                  ffn_w_out.astype(BF16))
    h = x
    for l in range(depth):
        final_nw = row(final_norm_w) if l == depth - 1 else None
        ffn_args = (l,) + ffn_stacks
        if l < n_a:
            w1_pad = jnp.pad(gla_w_gk1[l], ((0, 0), (0, LANES - GLA_GATE_RANK)))
            w_cat = jnp.concatenate([gla_w_qkvg[l], w1_pad], axis=1).astype(BF16)
            w2_pad = jnp.pad(gla_w_gk2[l], ((0, LANES - GLA_GATE_RANK), (0, 0))).astype(BF16)
            qkvg, gk = _gla_proj(h.reshape(B * S, D), row(attn_norm_w[l]), w_cat, w2_pad,
                                 row(gla_b_gk[l]))
            h = _gla_core(qkvg.reshape(B, S, -1), gk.reshape(B, S, dk), h, row(gla_onorm_w[l]),
                          gla_w_o[l].astype(BF16), dk, dv)
            h = _ffn(h, *ffn_args, final_nw=final_nw)
        else:
            j = l - n_a
            lam_init = 0.8 - 0.6 * math.exp(-0.3 * l)
            q3, k_new, vt_new = _qkv_proj(h, row(kv_norm_w), row(attn_norm_w[l]),
                                          w_kv[:, :n_qk].astype(BF16), w_kv[:, n_qk:].T.astype(BF16),
                                          diff_w_q[j].astype(BF16), cos, sin_lo, sin_hi,
                                          float(hd) ** -0.5 * LOG2E)
            if l == n_a:
                k3, vt4 = k_new, vt_new
            o3 = _diff_attn(q3, k3, vt4, diff_lambda[j], diff_subln_w[j].reshape(-1, 1), lam_init)
            h = _ffn(h, *ffn_args, mix3=o3, wo=diff_w_o[j].astype(BF16), final_nw=final_nw)
    return h
```
